```python
import jax, jax.numpy as jnp
from jax import lax
import numpy as np

D_MODEL = 1024
BATCH = 8
SEQ = 4096
DEPTH = 1

EPS = 1e-6
ROPE_THETA = 10000.0
BLOCK = 128

HEAD_DIM = 64
SWA_HEADS = 8
SWA_KV_HEADS = 2
SWA_GROUP = SWA_HEADS // SWA_KV_HEADS
WINDOW = 128

MLA_HEADS = 8
MLA_NOPE_DIM = 64
MLA_ROPE_DIM = 32
MLA_V_DIM = 64
MLA_QK_DIM = MLA_NOPE_DIM + MLA_ROPE_DIM
Q_LORA_RANK = 384
KV_LORA_RANK = 256

D_FF = -(-8 * D_MODEL // (3 * 256)) * 256

IN_SIZES = [
    SWA_HEADS * HEAD_DIM,
    SWA_KV_HEADS * HEAD_DIM,
    SWA_KV_HEADS * HEAD_DIM,
    Q_LORA_RANK,
    KV_LORA_RANK,
    MLA_ROPE_DIM,
    D_MODEL,
    D_MODEL,
]
IN_WIDTH = int(sum(IN_SIZES))
IN_OFFSETS = [int(v) for v in np.cumsum(IN_SIZES)[:-1]]

kernel_name = "hybrid_swa_sink_mla_gated_block"


def rmsnorm(x, g):
    xf = x.astype(jnp.float32)
    xf = xf * lax.rsqrt(jnp.mean(xf * xf, axis=-1, keepdims=True) + EPS)
    return (xf * g.astype(jnp.float32)).astype(x.dtype)


def rope_tables(seq, dim):
    inv = ROPE_THETA ** (-jnp.arange(0, dim, 2, dtype=jnp.float32) / dim)
    ang = jnp.arange(seq, dtype=jnp.float32)[:, None] * inv[None, :]
    return jnp.cos(ang)[:, None, :], jnp.sin(ang)[:, None, :]


def apply_rope(x, cos, sin):
    xf = x.astype(jnp.float32)
    x1, x2 = jnp.split(xf, 2, axis=-1)
    out = jnp.concatenate([x1 * cos - x2 * sin, x2 * cos + x1 * sin], axis=-1)
    return out.astype(x.dtype)


def swa_sink_attention(q, k, v, sinks):
    B, S = q.shape[0], q.shape[1]
    nb = S // BLOCK
    qb = q.reshape(B, nb, BLOCK, SWA_KV_HEADS, SWA_GROUP, HEAD_DIM)

    def band(t):
        tp = jnp.pad(t, ((0, 0), (BLOCK, 0), (0, 0), (0, 0)))
        tb = tp.reshape(B, nb + 1, BLOCK, SWA_KV_HEADS, HEAD_DIM)
        return jnp.concatenate([tb[:, :-1], tb[:, 1:]], axis=2)

    kw, vw = band(k), band(v)
    s = jnp.einsum('bnqhgd,bnkhd->bnhgqk', qb, kw,
                   preferred_element_type=jnp.float32) * (HEAD_DIM ** -0.5)
    qi = jnp.arange(BLOCK)[:, None]
    kj = jnp.arange(2 * BLOCK)[None, :]
    diff = qi - kj + BLOCK
    band_ok = (diff >= 0) & (diff < WINDOW)
    kpos = jnp.arange(nb)[:, None] * BLOCK + kj - BLOCK
    mask = band_ok[None] & (kpos >= 0)[:, None, :]
    s = jnp.where(mask[None, :, None, None], s, -jnp.inf)
    sink = sinks.astype(jnp.float32).reshape(1, 1, SWA_KV_HEADS, SWA_GROUP, 1, 1)
    m = jnp.maximum(jnp.max(s, axis=-1, keepdims=True), sink)
    p = jnp.exp(s - m)
    denom = jnp.sum(p, axis=-1, keepdims=True) + jnp.exp(sink - m)
    p = (p / denom).astype(v.dtype)
    o = jnp.einsum('bnhgqk,bnkhd->bnqhgd', p, vw)
    return o.reshape(B, S, SWA_HEADS * HEAD_DIM)


def mla_attention(q_nope, q_rope, k_nope, k_rope, v):
    B, S = q_nope.shape[0], q_nope.shape[1]
    nb = S // BLOCK
    scale = MLA_QK_DIM ** -0.5
    kpos = jnp.arange(S)

    def to_blocks(t):
        return jnp.moveaxis(t.reshape(B, nb, BLOCK, *t.shape[2:]), 1, 0)

    def one_block(args):
        qn, qr, n = args
        s = (jnp.einsum('bqhd,bkhd->bhqk', qn, k_nope, preferred_element_type=jnp.float32)
             + jnp.einsum('bqhr,bkr->bhqk', qr, k_rope, preferred_element_type=jnp.float32)) * scale
        qpos = n * BLOCK + jnp.arange(BLOCK)
        causal = kpos[None, :] <= qpos[:, None]
        s = jnp.where(causal[None, None], s, -jnp.inf)
        p = jax.nn.softmax(s, axis=-1).astype(v.dtype)
        return jnp.einsum('bhqk,bkhd->bqhd', p, v)

    o = lax.map(one_block, (to_blocks(q_nope), to_blocks(q_rope), jnp.arange(nb)))
    return jnp.moveaxis(o, 0, 1).reshape(B, S, MLA_HEADS * MLA_V_DIM)


def setup_inputs(seed: int = 0) -> dict:
    key = jax.random.key(seed)
    ks = jax.random.split(key, 17)
    f32 = jnp.float32

    def w(k, shape, fan_in):
        return jax.random.normal(k, shape, f32) * (fan_in ** -0.5)

    def gain(k, shape):
        return 1.0 + 0.02 * jax.random.normal(k, shape, f32)

    L = DEPTH
    return {
        "x": jax.random.normal(ks[0], (BATCH, SEQ, D_MODEL), f32),
        "mix_norm_g": gain(ks[1], (L, D_MODEL)),
        "w_in": w(ks[2], (L, D_MODEL, IN_WIDTH), D_MODEL),
        "swa_sinks": 0.5 * jax.random.normal(ks[3], (L, SWA_HEADS), f32),
        "q_norm_g": gain(ks[4], (L, Q_LORA_RANK)),
        "w_uq": w(ks[5], (L, Q_LORA_RANK, MLA_HEADS * MLA_QK_DIM), Q_LORA_RANK),
        "kv_norm_g": gain(ks[6], (L, KV_LORA_RANK)),
        "w_ukv": w(ks[7], (L, KV_LORA_RANK, MLA_HEADS * (MLA_NOPE_DIM + MLA_V_DIM)), KV_LORA_RANK),
        "w_o_swa": w(ks[8], (L, SWA_HEADS * HEAD_DIM, D_MODEL), SWA_HEADS * HEAD_DIM),
        "w_o_mla": w(ks[9], (L, MLA_HEADS * MLA_V_DIM, D_MODEL), MLA_HEADS * MLA_V_DIM),
        "w_out": w(ks[10], (L, D_MODEL, D_MODEL), D_MODEL),
        "ffn_norm_g": gain(ks[11], (L, D_MODEL)),
        "w_gate": w(ks[12], (L, D_MODEL, D_FF), D_MODEL),
        "w_up": w(ks[13], (L, D_MODEL, D_FF), D_MODEL),
        "w_down": w(ks[14], (L, D_FF, D_MODEL), D_FF),
        "final_norm_g": gain(ks[15], (D_MODEL,)),
    }


def reference(x, mix_norm_g, w_in, swa_sinks, q_norm_g, w_uq, kv_norm_g, w_ukv,
              w_o_swa, w_o_mla, w_out, ffn_norm_g, w_gate, w_up, w_down, final_norm_g):
    B, S = x.shape[0], x.shape[1]
    cos_a, sin_a = rope_tables(S, HEAD_DIM)
    cos_b, sin_b = rope_tables(S, MLA_ROPE_DIM)

    for l in range(DEPTH):
        h = rmsnorm(x, mix_norm_g[l])
        proj = h @ w_in[l]
        qa, ka, va, q_lat, kv_lat, k_r, g_a, g_b = jnp.split(proj, IN_OFFSETS, axis=-1)

        qa = apply_rope(qa.reshape(B, S, SWA_HEADS, HEAD_DIM), cos_a, sin_a)
        ka = apply_rope(ka.reshape(B, S, SWA_KV_HEADS, HEAD_DIM), cos_a, sin_a)
        va = va.reshape(B, S, SWA_KV_HEADS, HEAD_DIM)
        o_a = swa_sink_attention(qa, ka, va, swa_sinks[l])

        cq = rmsnorm(q_lat, q_norm_g[l])
        qb = (cq @ w_uq[l]).reshape(B, S, MLA_HEADS, MLA_QK_DIM)
        q_nope, q_rope = jnp.split(qb, [MLA_NOPE_DIM], axis=-1)
        q_rope = apply_rope(q_rope, cos_b, sin_b)
        ckv = rmsnorm(kv_lat, kv_norm_g[l])
        kvb = (ckv @ w_ukv[l]).reshape(B, S, MLA_HEADS, MLA_NOPE_DIM + MLA_V_DIM)
        k_nope, vb = jnp.split(kvb, [MLA_NOPE_DIM], axis=-1)
        k_rope = apply_rope(k_r[:, :, None, :], cos_b, sin_b)[:, :, 0, :]
        o_b = mla_attention(q_nope, q_rope, k_nope, k_rope, vb)

        y = jax.nn.sigmoid(g_a) * (o_a @ w_o_swa[l]) + jax.nn.sigmoid(g_b) * (o_b @ w_o_mla[l])
        x = x + y @ w_out[l]

        h = rmsnorm(x, ffn_norm_g[l])
        x = x + (jax.nn.silu(h @ w_gate[l]) * (h @ w_up[l])) @ w_down[l]

    return rmsnorm(x, final_norm_g)
```

```python
import functools

import jax
import jax.numpy as jnp
import numpy as np
from jax import lax
from jax.experimental import pallas as pl
from jax.experimental.pallas import tpu as pltpu

EPS = 1e-6
ROPE_THETA = 10000.0
BLOCK = 128
HEAD_DIM = 64
SWA_HEADS = 8
SWA_KV_HEADS = 2
SWA_GROUP = SWA_HEADS // SWA_KV_HEADS
MLA_HEADS = 8
MLA_NOPE_DIM = 64
MLA_ROPE_DIM = 32
MLA_V_DIM = 64
MLA_QK_DIM = MLA_NOPE_DIM + MLA_ROPE_DIM
Q_LORA_RANK = 384
KV_LORA_RANK = 256

LANES = 128
V7X_VMEM_BYTES = 64 * 1024 * 1024
VMEM_LIMIT_BYTES = 56 * 1024 * 1024

PROJ_ROWS = 512
ATTN_Q_ROWS = 512
MLA_K_ROWS = 512
POST_ROWS = 256

MLA_HEAD_PAD = LANES
SWA_Q_ORDER = (0, 4, 1, 5, 2, 6, 3, 7)

F32 = jnp.float32
BF16 = jnp.bfloat16


def _rmsnorm_f32(x, g):
    return x * lax.rsqrt(jnp.mean(x * x, axis=-1, keepdims=True) + EPS) * g


def _rope_lane_group(x, tab_ref, half):
    up = pltpu.roll(x, LANES - half, axis=1)
    down = pltpu.roll(x, half, axis=1)
    return x * tab_ref[0] + up * tab_ref[1] + down * tab_ref[2]


def _in_proj_kernel(x_ref, g_ref, win_ref, qg_ref, wuq_ref, kvg_ref, wukv_ref,
                    swa_tab, mq_tab, mk_tab,
                    qa_ref, ka_ref, va_ref, qm_ref, km_ref, vm_ref):
    h = _rmsnorm_f32(x_ref[...], g_ref[...]).astype(BF16)
    p = jnp.dot(h, win_ref[...], preferred_element_type=F32)

    o_q, o_k, o_v = 0, 512, 640
    o_ql, o_kvl, o_kr = 768, 768 + Q_LORA_RANK, 768 + Q_LORA_RANK + KV_LORA_RANK

    for c in range(4):
        qa_ref[:, c * LANES:(c + 1) * LANES] = _rope_lane_group(
            p[:, o_q + c * LANES:o_q + (c + 1) * LANES], swa_tab, HEAD_DIM // 2).astype(BF16)
    ka_ref[...] = _rope_lane_group(p[:, o_k:o_k + LANES], swa_tab, HEAD_DIM // 2).astype(BF16)
    va_ref[...] = p[:, o_v:o_v + LANES].astype(BF16)

    cq = _rmsnorm_f32(p[:, o_ql:o_ql + Q_LORA_RANK], qg_ref[...]).astype(BF16)
    qb = jnp.dot(cq, wuq_ref[...], preferred_element_type=F32)
    for c in range(MLA_HEADS):
        qm_ref[:, c * LANES:(c + 1) * LANES] = _rope_lane_group(
            qb[:, c * LANES:(c + 1) * LANES], mq_tab, MLA_ROPE_DIM // 2).astype(BF16)

    ckv = _rmsnorm_f32(p[:, o_kvl:o_kvl + KV_LORA_RANK], kvg_ref[...]).astype(BF16)
    kvb = jnp.dot(ckv, wukv_ref[...], preferred_element_type=F32)
    kr = _rope_lane_group(p[:, o_kr:o_kr + LANES], mk_tab, MLA_ROPE_DIM // 2)
    for c in range(MLA_HEADS):
        km_ref[:, c * LANES:(c + 1) * LANES] = (kvb[:, c * LANES:(c + 1) * LANES] + kr).astype(BF16)
    vm_ref[...] = kvb[:, MLA_HEADS * LANES:].astype(BF16)


def _in_proj(x, g, win, qg, wuq, kvg, wukv, swa_tab, mq_tab, mk_tab):
    B, S, D = x.shape
    rows = PROJ_ROWS
    ns = S // rows
    tok = lambda w: pl.BlockSpec((None, rows, w), lambda si, b: (b, si, 0))
    const = lambda shape: pl.BlockSpec(shape, lambda si, b: (0,) * len(shape))
    tab = pl.BlockSpec((3, rows, LANES), lambda si, b: (0, si, 0))
    out_w = (512, LANES, LANES, MLA_HEADS * LANES, MLA_HEADS * LANES, MLA_HEADS * MLA_V_DIM)
    return pl.pallas_call(
        _in_proj_kernel,
        grid=(ns, B),
        in_specs=[tok(D), const(g.shape), const(win.shape), const(qg.shape), const(wuq.shape),
                  const(kvg.shape), const(wukv.shape), tab, tab, tab],
        out_specs=[tok(w) for w in out_w],
        out_shape=[jax.ShapeDtypeStruct((B, S, w), BF16) for w in out_w],
        compiler_params=pltpu.CompilerParams(
            dimension_semantics=("arbitrary", "arbitrary"), vmem_limit_bytes=VMEM_LIMIT_BYTES),
        name="in_proj",
    )(x, g, win, qg, wuq, kvg, wukv, swa_tab, mq_tab, mk_tab)


def _swa_kernel(sink_ref, q_ref, k_ref, v_ref, o_ref):
    n = pl.program_id(1)
    lane = lax.broadcasted_iota(jnp.int32, (BLOCK, LANES), 1)
    low = lane < HEAD_DIM
    qi = lax.broadcasted_iota(jnp.int32, (BLOCK, 2 * BLOCK), 0)
    kj = lax.broadcasted_iota(jnp.int32, (BLOCK, 2 * BLOCK), 1)
    diff = qi - kj + BLOCK
    band = (diff >= 0) & (diff < BLOCK)

    for j in range(ATTN_Q_ROWS // BLOCK):
        blk = n * (ATTN_Q_ROWS // BLOCK) + j
        prev = jnp.maximum(blk - 1, 0)
        cur0 = pl.multiple_of(blk * BLOCK, BLOCK)
        prev0 = pl.multiple_of(prev * BLOCK, BLOCK)
        k = jnp.concatenate([k_ref[pl.ds(prev0, BLOCK), :], k_ref[pl.ds(cur0, BLOCK), :]], axis=0)
        v = jnp.concatenate([v_ref[pl.ds(prev0, BLOCK), :], v_ref[pl.ds(cur0, BLOCK), :]], axis=0)
        mask = band & ((kj >= BLOCK) | (blk > 0))

        zero = jnp.zeros((BLOCK, LANES), BF16)
        qs = []
        for c in range(4):
            qc = q_ref[j * BLOCK:(j + 1) * BLOCK, c * LANES:(c + 1) * LANES]
            qs.append(jnp.where(low, qc, zero))
            qs.append(jnp.where(low, zero, qc))
        q8 = jnp.concatenate(qs, axis=0)
        s = lax.dot_general(q8, k, (((1,), (1,)), ((), ())), preferred_element_type=F32)

        ps, inv = [], []
        for i in range(2 * 4):
            head = (i // 2) + 4 * (i % 2)
            sink = sink_ref[head]
            si = jnp.where(mask, s[i * BLOCK:(i + 1) * BLOCK], -jnp.inf)
            m = jnp.maximum(jnp.max(si, axis=-1, keepdims=True), sink)
            pi = jnp.exp(si - m)
            denom = jnp.sum(pi, axis=-1, keepdims=True) + jnp.exp(sink - m)
            ps.append(pi.astype(BF16))
            inv.append(1.0 / denom)
        p8 = jnp.concatenate(ps, axis=0)
        o8 = jnp.dot(p8, v, preferred_element_type=F32)
        for c in range(4):
            oa = o8[(2 * c) * BLOCK:(2 * c + 1) * BLOCK] * inv[2 * c]
            ob = o8[(2 * c + 1) * BLOCK:(2 * c + 2) * BLOCK] * inv[2 * c + 1]
            o_ref[j * BLOCK:(j + 1) * BLOCK, c * LANES:(c + 1) * LANES] = jnp.where(low, oa, ob).astype(BF16)


def _swa_attention(sinks, qa, ka, va):
    B, S, _ = qa.shape
    rows = ATTN_Q_ROWS
    return pl.pallas_call(
        _swa_kernel,
        grid=(B, S // rows),
        in_specs=[pl.BlockSpec(memory_space=pltpu.SMEM),
                  pl.BlockSpec((None, rows, 512), lambda b, n: (b, n, 0)),
                  pl.BlockSpec((None, S, LANES), lambda b, n: (b, 0, 0)),
                  pl.BlockSpec((None, S, LANES), lambda b, n: (b, 0, 0))],
        out_specs=pl.BlockSpec((None, rows, 512), lambda b, n: (b, n, 0)),
        out_shape=jax.ShapeDtypeStruct((B, S, 512), BF16),
        compiler_params=pltpu.CompilerParams(
            dimension_semantics=("arbitrary", "arbitrary"), vmem_limit_bytes=VMEM_LIMIT_BYTES),
        name="swa_attention",
    )(sinks, qa, ka, va)


def _mla_kernel(q_ref, k_ref, v_ref, o_ref, m_ref, l_ref, acc_ref):
    n = pl.program_id(1)
    tq, tk = ATTN_Q_ROWS, MLA_K_ROWS

    m_ref[...] = jnp.full(m_ref.shape, -jnp.inf, F32)
    l_ref[...] = jnp.zeros(l_ref.shape, F32)
    acc_ref[...] = jnp.zeros(acc_ref.shape, F32)

    def chunk(j, masked):
        k0 = pl.multiple_of(j * tk, tk)
        if masked:
            r = lax.broadcasted_iota(jnp.int32, (tq, tk), 0)
            c = lax.broadcasted_iota(jnp.int32, (tq, tk), 1)
            keep = c <= r
        for h in range(MLA_HEADS):
            q = q_ref[:, h * LANES:(h + 1) * LANES]
            k = k_ref[pl.ds(k0, tk), h * LANES:(h + 1) * LANES]
            s = lax.dot_general(q, k, (((1,), (1,)), ((), ())), preferred_element_type=F32)
            if masked:
                s = jnp.where(keep, s, -jnp.inf)
            m_prev = m_ref[h]
            m_new = jnp.maximum(m_prev, jnp.max(s, axis=-1, keepdims=True))
            alpha = jnp.exp(m_prev - m_new)
            p = jnp.exp(s - m_new)
            l_ref[h] = alpha * l_ref[h] + jnp.sum(p, axis=-1, keepdims=True)
            m_ref[h] = m_new
            pair = (h // 2) * LANES
            v = v_ref[pl.ds(k0, tk), pair:pair + LANES]
            acc_ref[h] = alpha * acc_ref[h] + jnp.dot(p.astype(BF16), v, preferred_element_type=F32)

    def body(j, carry):
        chunk(j, False)
        return carry

    lax.fori_loop(0, n, body, 0)
    chunk(n, True)

    low = lax.broadcasted_iota(jnp.int32, (tq, LANES), 1) < MLA_V_DIM
    for c in range(MLA_HEADS // 2):
        even = acc_ref[2 * c] / l_ref[2 * c]
        odd = acc_ref[2 * c + 1] / l_ref[2 * c + 1]
        o_ref[:, c * LANES:(c + 1) * LANES] = jnp.where(low, even, odd).astype(BF16)


def _mla_attention(qm, km, vm):
    B, S, _ = qm.shape
    rows = ATTN_Q_ROWS
    assert rows == MLA_K_ROWS
    return pl.pallas_call(
        _mla_kernel,
        grid=(B, S // rows),
        in_specs=[pl.BlockSpec((None, rows, MLA_HEADS * LANES), lambda b, n: (b, n, 0)),
                  pl.BlockSpec((None, S, MLA_HEADS * LANES), lambda b, n: (b, 0, 0)),
                  pl.BlockSpec((None, S, MLA_HEADS * MLA_V_DIM), lambda b, n: (b, 0, 0))],
        out_specs=pl.BlockSpec((None, rows, MLA_HEADS * MLA_V_DIM), lambda b, n: (b, n, 0)),
        out_shape=jax.ShapeDtypeStruct((B, S, MLA_HEADS * MLA_V_DIM), BF16),
        scratch_shapes=[pltpu.VMEM((MLA_HEADS, rows, 1), F32),
                        pltpu.VMEM((MLA_HEADS, rows, 1), F32),
                        pltpu.VMEM((MLA_HEADS, rows, LANES), F32)],
        compiler_params=pltpu.CompilerParams(
            dimension_semantics=("arbitrary", "arbitrary"), vmem_limit_bytes=VMEM_LIMIT_BYTES),
        name="mla_attention",
    )(qm, km, vm)


def _post_kernel(x_ref, oa_ref, ob_ref, g1_ref, wg_ref, woa_ref, wob_ref, wout_ref,
                 g2_ref, wgate_ref, wup_ref, wdown_ref, g3_ref, out_ref, *, final_norm):
    D = x_ref.shape[-1]
    x = x_ref[...]
    h = _rmsnorm_f32(x, g1_ref[...]).astype(BF16)
    gates = jnp.dot(h, wg_ref[...], preferred_element_type=F32)
    ya = jnp.dot(oa_ref[...], woa_ref[...], preferred_element_type=F32)
    yb = jnp.dot(ob_ref[...], wob_ref[...], preferred_element_type=F32)
    y = jax.nn.sigmoid(gates[:, :D]) * ya + jax.nn.sigmoid(gates[:, D:]) * yb
    x1 = x + jnp.dot(y.astype(BF16), wout_ref[...], preferred_element_type=F32)

    h2 = _rmsnorm_f32(x1, g2_ref[...]).astype(BF16)
    hg = jnp.dot(h2, wgate_ref[...], preferred_element_type=F32)
    hu = jnp.dot(h2, wup_ref[...], preferred_element_type=F32)
    a = (hg * jax.nn.sigmoid(hg) * hu).astype(BF16)
    x2 = x1 + jnp.dot(a, wdown_ref[...], preferred_element_type=F32)
    out_ref[...] = _rmsnorm_f32(x2, g3_ref[...]) if final_norm else x2


def _post(x, oa, ob, g1, wg, woa, wob, wout, g2, wgate, wup, wdown, g3, final_norm):
    B, S, D = x.shape
    rows = POST_ROWS
    tok = lambda w: pl.BlockSpec((None, rows, w), lambda b, i: (b, i, 0))
    const = lambda a: pl.BlockSpec(a.shape, lambda b, i: (0,) * a.ndim, pipeline_mode=pl.Buffered(1))
    return pl.pallas_call(
        functools.partial(_post_kernel, final_norm=final_norm),
        grid=(B, S // rows),
        in_specs=[tok(D), tok(oa.shape[-1]), tok(ob.shape[-1])]
                 + [const(a) for a in (g1, wg, woa, wob, wout, g2, wgate, wup, wdown, g3)],
        out_specs=tok(D),
        out_shape=jax.ShapeDtypeStruct((B, S, D), F32),
        compiler_params=pltpu.CompilerParams(
            dimension_semantics=("arbitrary", "arbitrary"), vmem_limit_bytes=VMEM_LIMIT_BYTES),
        name="post_attention",
    )(x, oa, ob, g1, wg, woa, wob, wout, g2, wgate, wup, wdown, g3)


def _rope_table(seq, dim, group_offsets, scale):
    half = dim // 2
    inv = ROPE_THETA ** (-jnp.arange(0, dim, 2, dtype=F32) / dim)
    ang = jnp.arange(seq, dtype=F32)[:, None] * inv[None, :]
    cos, sin = jnp.cos(ang), jnp.sin(ang)
    c = jnp.ones((seq, LANES), F32)
    s1 = jnp.zeros((seq, LANES), F32)
    s2 = jnp.zeros((seq, LANES), F32)
    for o in group_offsets:
        c = c.at[:, o:o + half].set(cos).at[:, o + half:o + dim].set(cos)
        s1 = s1.at[:, o:o + half].set(-sin)
        s2 = s2.at[:, o + half:o + dim].set(sin)
    return jnp.stack([c, s1, s2]) * scale


def kernel(x, mix_norm_g, w_in, swa_sinks, q_norm_g, w_uq, kv_norm_g, w_ukv, w_o_swa, w_o_mla,
           w_out, ffn_norm_g, w_gate, w_up, w_down, final_norm_g):
    B, S, D = x.shape
    depth = w_in.shape[0]
    assert S % ATTN_Q_ROWS == 0 and S % PROJ_ROWS == 0 and S % POST_ROWS == 0

    swa_tab = _rope_table(S, HEAD_DIM, (0, HEAD_DIM), 1.0)
    mk_tab = _rope_table(S, MLA_ROPE_DIM, (MLA_NOPE_DIM,), 1.0)
    mq_tab = mk_tab * (MLA_QK_DIM ** -0.5)
    row = lambda v: v.reshape(1, -1)

    for l in range(depth):
        wi = w_in[l]
        wq = wi[:, :512].reshape(D, SWA_HEADS, HEAD_DIM)[:, jnp.array(SWA_Q_ORDER)].reshape(D, 512) * (HEAD_DIM ** -0.5)
        wkr = jnp.pad(wi[:, 1408:1440], ((0, 0), (MLA_NOPE_DIM, LANES - MLA_QK_DIM)))
        win = jnp.concatenate([wq, wi[:, 512:1408], wkr], axis=1).astype(BF16)
        wg = wi[:, 1440:].astype(BF16)
        wuq = jnp.pad(w_uq[l].reshape(Q_LORA_RANK, MLA_HEADS, MLA_QK_DIM),
                      ((0, 0), (0, 0), (0, LANES - MLA_QK_DIM))).reshape(Q_LORA_RANK, -1).astype(BF16)
        wkv = w_ukv[l].reshape(KV_LORA_RANK, MLA_HEADS, MLA_NOPE_DIM + MLA_V_DIM)
        wuk = jnp.pad(wkv[:, :, :MLA_NOPE_DIM], ((0, 0), (0, 0), (0, LANES - MLA_NOPE_DIM)))
        wukv = jnp.concatenate([wuk.reshape(KV_LORA_RANK, -1),
                                wkv[:, :, MLA_NOPE_DIM:].reshape(KV_LORA_RANK, -1)], axis=1).astype(BF16)
        woa = w_o_swa[l].reshape(SWA_HEADS, HEAD_DIM, D)[jnp.array(SWA_Q_ORDER)].reshape(-1, D).astype(BF16)

        qa, ka, va, qm, km, vm = _in_proj(x, row(mix_norm_g[l]), win, row(q_norm_g[l]), wuq,
                                          row(kv_norm_g[l]), wukv, swa_tab, mq_tab, mk_tab)
        oa = _swa_attention(swa_sinks[l], qa, ka, va)
        ob = _mla_attention(qm, km, vm)
        x = _post(x, oa, ob, row(mix_norm_g[l]), wg, woa, w_o_mla[l].astype(BF16), w_out[l].astype(BF16),
                  row(ffn_norm_g[l]), w_gate[l].astype(BF16), w_up[l].astype(BF16),
                  w_down[l].astype(BF16), row(final_norm_g), final_norm=(l == depth - 1))
    return x
```

```python
import functools
import math

import jax
import jax.numpy as jnp
from jax import lax
from jax.experimental import pallas as pl
from jax.experimental.pallas import tpu as pltpu

EPS = 1e-6
ROPE_THETA = 10000.0
BLOCK = 128
HEAD_DIM = 64
SWA_HEADS = 8
SWA_KV_HEADS = 2
SWA_GROUP = SWA_HEADS // SWA_KV_HEADS
MLA_HEADS = 8
MLA_NOPE_DIM = 64
MLA_ROPE_DIM = 32
MLA_V_DIM = 64
MLA_QK_DIM = MLA_NOPE_DIM + MLA_ROPE_DIM
Q_LORA_RANK = 384
KV_LORA_RANK = 256

LANES = 128
V7X_VMEM_BYTES = 64 * 1024 * 1024
VMEM_LIMIT_BYTES = 56 * 1024 * 1024

PROJ_ROWS = 512
SWA_Q_ROWS = 512
MLA_Q_COLS = 512
MLA_K_ROWS = 256
POST_ROWS = 256
MLA_LOOKAHEAD = 2

MLA_HEAD_PAD = LANES
MLA_ROPE_HALF = MLA_ROPE_DIM // 2
SWA_Q_ORDER = (0, 4, 1, 5, 2, 6, 3, 7)
MLA_Q_SCALE = (MLA_QK_DIM ** -0.5) * math.log2(math.e)

F32 = jnp.float32
BF16 = jnp.bfloat16
NT_DIMS = (((1,), (1,)), ((), ()))


def _rmsnorm_f32(x, g):
    return x * lax.rsqrt(jnp.mean(x * x, axis=-1, keepdims=True) + EPS) * g


def _rope_lane_group(x, tab_ref, half):
    up = pltpu.roll(x, LANES - half, axis=1)
    down = pltpu.roll(x, half, axis=1)
    return x * tab_ref[0] + up * tab_ref[1] + down * tab_ref[2]


def _in_proj_kernel(x_ref, g_ref, win_ref, qg_ref, wuqt_ref, kvg_ref, wuk_ref, wuvt_ref,
                    swa_tab, mk_tab, mqt_tab,
                    qa_ref, ka_ref, va_ref, qt_ref, km_ref, vt_ref):
    h = _rmsnorm_f32(x_ref[...], g_ref[...]).astype(BF16)
    p = jnp.dot(h, win_ref[...], preferred_element_type=F32)

    o_q, o_k, o_v = 0, 512, 640
    o_ql, o_kvl, o_kr = 768, 768 + Q_LORA_RANK, 768 + Q_LORA_RANK + KV_LORA_RANK

    for c in range(4):
        qa_ref[:, c * LANES:(c + 1) * LANES] = _rope_lane_group(
            p[:, o_q + c * LANES:o_q + (c + 1) * LANES], swa_tab, HEAD_DIM // 2).astype(BF16)
    ka_ref[...] = _rope_lane_group(p[:, o_k:o_k + LANES], swa_tab, HEAD_DIM // 2).astype(BF16)
    va_ref[...] = p[:, o_v:o_v + LANES].astype(BF16)

    cq = _rmsnorm_f32(p[:, o_ql:o_ql + Q_LORA_RANK], qg_ref[...]).astype(BF16)
    qt = lax.dot_general(wuqt_ref[...], cq, NT_DIMS, preferred_element_type=F32)
    cos_q, sin_q = mqt_tab[0], mqt_tab[1]
    r0, r1, r2 = MLA_NOPE_DIM, MLA_NOPE_DIM + MLA_ROPE_HALF, MLA_QK_DIM
    for hd in range(MLA_HEADS):
        qh = qt[hd * LANES:(hd + 1) * LANES]
        x1, x2 = qh[r0:r1], qh[r1:r2]
        qt_ref[hd] = jnp.concatenate(
            [qh[:r0] * MLA_Q_SCALE, x1 * cos_q - x2 * sin_q, x2 * cos_q + x1 * sin_q, qh[r2:]],
            axis=0).astype(BF16)

    ckv = _rmsnorm_f32(p[:, o_kvl:o_kvl + KV_LORA_RANK], kvg_ref[...]).astype(BF16)
    kb = jnp.dot(ckv, wuk_ref[...], preferred_element_type=F32)
    kr = _rope_lane_group(p[:, o_kr:o_kr + LANES], mk_tab, MLA_ROPE_HALF)
    for hd in range(MLA_HEADS):
        km_ref[hd] = (kb[:, hd * LANES:(hd + 1) * LANES] + kr).astype(BF16)
    vt = lax.dot_general(wuvt_ref[...], ckv, NT_DIMS, preferred_element_type=F32)
    for hd in range(MLA_HEADS):
        for t in range(vt_ref.shape[1]):
            vt_ref[hd, t] = vt[hd * MLA_V_DIM:(hd + 1) * MLA_V_DIM,
                               t * MLA_K_ROWS:(t + 1) * MLA_K_ROWS].astype(BF16)


def _in_proj(x, g, win, qg, wuqt, kvg, wuk, wuvt, swa_tab, mk_tab, mqt_tab):
    B, S, D = x.shape
    rows = PROJ_ROWS
    ns = S // rows
    kc = rows // MLA_K_ROWS
    H = MLA_HEADS
    tok = lambda w: pl.BlockSpec((None, rows, w), lambda si, b: (b, si, 0))
    const = lambda a: pl.BlockSpec(a.shape, lambda si, b: (0,) * a.ndim)
    tab = pl.BlockSpec((3, rows, LANES), lambda si, b: (0, si, 0))
    tabt = pl.BlockSpec((2, MLA_ROPE_HALF, rows), lambda si, b: (0, 0, si))
    return pl.pallas_call(
        _in_proj_kernel,
        grid=(ns, B),
        in_specs=[tok(D), const(g), const(win), const(qg), const(wuqt), const(kvg), const(wuk),
                  const(wuvt), tab, tab, tabt],
        out_specs=[tok(512), tok(LANES), tok(LANES),
                   pl.BlockSpec((None, H, LANES, rows), lambda si, b: (b, 0, 0, si)),
                   pl.BlockSpec((None, H, rows, LANES), lambda si, b: (b, 0, si, 0)),
                   pl.BlockSpec((None, H, kc, MLA_V_DIM, MLA_K_ROWS), lambda si, b: (b, 0, si, 0, 0))],
        out_shape=[jax.ShapeDtypeStruct((B, S, 512), BF16),
                   jax.ShapeDtypeStruct((B, S, LANES), BF16),
                   jax.ShapeDtypeStruct((B, S, LANES), BF16),
                   jax.ShapeDtypeStruct((B, H, LANES, S), BF16),
                   jax.ShapeDtypeStruct((B, H, S, LANES), BF16),
                   jax.ShapeDtypeStruct((B, H, S // MLA_K_ROWS, MLA_V_DIM, MLA_K_ROWS), BF16)],
        compiler_params=pltpu.CompilerParams(
            dimension_semantics=("arbitrary", "arbitrary"), vmem_limit_bytes=VMEM_LIMIT_BYTES),
        name="in_proj",
    )(x, g, win, qg, wuqt, kvg, wuk, wuvt, swa_tab, mk_tab, mqt_tab)


def _swa_kernel(sink_ref, q_ref, k_ref, v_ref, o_ref):
    n = pl.program_id(1)
    lane = lax.broadcasted_iota(jnp.int32, (BLOCK, LANES), 1)
    low = lane < HEAD_DIM
    qi = lax.broadcasted_iota(jnp.int32, (BLOCK, 2 * BLOCK), 0)
    kj = lax.broadcasted_iota(jnp.int32, (BLOCK, 2 * BLOCK), 1)
    diff = qi - kj + BLOCK
    band = (diff >= 0) & (diff < BLOCK)

    for j in range(SWA_Q_ROWS // BLOCK):
        blk = n * (SWA_Q_ROWS // BLOCK) + j
        prev = jnp.maximum(blk - 1, 0)
        cur0 = pl.multiple_of(blk * BLOCK, BLOCK)
        prev0 = pl.multiple_of(prev * BLOCK, BLOCK)
        k = jnp.concatenate([k_ref[pl.ds(prev0, BLOCK), :], k_ref[pl.ds(cur0, BLOCK), :]], axis=0)
        v = jnp.concatenate([v_ref[pl.ds(prev0, BLOCK), :], v_ref[pl.ds(cur0, BLOCK), :]], axis=0)
        mask = band & ((kj >= BLOCK) | (blk > 0))

        zero = jnp.zeros((BLOCK, LANES), BF16)
        qs = []
        for c in range(4):
            qc = q_ref[j * BLOCK:(j + 1) * BLOCK, c * LANES:(c + 1) * LANES]
            qs.append(jnp.where(low, qc, zero))
            qs.append(jnp.where(low, zero, qc))
        q8 = jnp.concatenate(qs, axis=0)
        s = lax.dot_general(q8, k, NT_DIMS, preferred_element_type=F32)

        ps, inv = [], []
        for i in range(2 * 4):
            head = (i // 2) + 4 * (i % 2)
            sink = sink_ref[head]
            si = jnp.where(mask, s[i * BLOCK:(i + 1) * BLOCK], -jnp.inf)
            m = jnp.maximum(jnp.max(si, axis=-1, keepdims=True), sink)
            pi = jnp.exp(si - m)
            denom = jnp.sum(pi, axis=-1, keepdims=True) + jnp.exp(sink - m)
            ps.append(pi.astype(BF16))
            inv.append(1.0 / denom)
        p8 = jnp.concatenate(ps, axis=0)
        o8 = jnp.dot(p8, v, preferred_element_type=F32)
        for c in range(4):
            oa = o8[(2 * c) * BLOCK:(2 * c + 1) * BLOCK] * inv[2 * c]
            ob = o8[(2 * c + 1) * BLOCK:(2 * c + 2) * BLOCK] * inv[2 * c + 1]
            o_ref[j * BLOCK:(j + 1) * BLOCK, c * LANES:(c + 1) * LANES] = jnp.where(low, oa, ob).astype(BF16)


def _swa_attention(sinks, qa, ka, va):
    B, S, _ = qa.shape
    rows = SWA_Q_ROWS
    return pl.pallas_call(
        _swa_kernel,
        grid=(B, S // rows),
        in_specs=[pl.BlockSpec(memory_space=pltpu.SMEM),
                  pl.BlockSpec((None, rows, 512), lambda b, n: (b, n, 0)),
                  pl.BlockSpec((None, S, LANES), lambda b, n: (b, 0, 0)),
                  pl.BlockSpec((None, S, LANES), lambda b, n: (b, 0, 0))],
        out_specs=pl.BlockSpec((None, rows, 512), lambda b, n: (b, n, 0)),
        out_shape=jax.ShapeDtypeStruct((B, S, 512), BF16),
        compiler_params=pltpu.CompilerParams(
            dimension_semantics=("arbitrary", "arbitrary"), vmem_limit_bytes=VMEM_LIMIT_BYTES),
        name="swa_attention",
    )(sinks, qa, ka, va)


def _mla_kernel(qt_ref, k_ref, vt_ref, o_ref, m_ref, l_ref, acc_ref):
    n = pl.program_id(1)
    tq, tk = MLA_Q_COLS, MLA_K_ROWS

    m_ref[...] = jnp.full(m_ref.shape, -jnp.inf, F32)
    l_ref[...] = jnp.zeros(l_ref.shape, F32)
    acc_ref[...] = jnp.zeros(acc_ref.shape, F32)

    def chunk(j, masked):
        k0 = pl.multiple_of(j * tk, tk)
        if masked:
            key = lax.broadcasted_iota(jnp.int32, (tk, tq), 0)
            qry = lax.broadcasted_iota(jnp.int32, (tk, tq), 1)
            keep = key + (j * tk - n * tq) <= qry

        def scores(h):
            return jnp.dot(k_ref[h, pl.ds(k0, tk), :], qt_ref[h], preferred_element_type=F32)

        def weighted_values(h, p, alpha):
            acc_ref[h] = alpha * acc_ref[h] + jnp.dot(vt_ref[h, j], p, preferred_element_type=F32)

        ahead = [scores(h) for h in range(MLA_LOOKAHEAD)]
        pending = None
        for h in range(MLA_HEADS):
            if h + MLA_LOOKAHEAD < MLA_HEADS:
                ahead.append(scores(h + MLA_LOOKAHEAD))
            s = ahead[h]
            if masked:
                s = jnp.where(keep, s, -jnp.inf)
            m_prev = m_ref[h]
            m_new = jnp.maximum(m_prev, jnp.max(s, axis=0, keepdims=True))
            alpha = jnp.exp2(m_prev - m_new)
            p = jnp.exp2(s - m_new)
            l_ref[h] = alpha * l_ref[h] + jnp.sum(p, axis=0, keepdims=True)
            m_ref[h] = m_new
            if pending is not None:
                weighted_values(*pending)
            pending = (h, p.astype(BF16), alpha)
        weighted_values(*pending)

    def body(j, carry):
        chunk(j, False)
        return carry

    full = n * (tq // tk)
    lax.fori_loop(0, full, body, 0)
    for d in range(tq // tk):
        chunk(full + d, True)

    ot = jnp.concatenate([acc_ref[h] / l_ref[h] for h in range(MLA_HEADS)], axis=0)
    o_ref[...] = ot.T.astype(BF16)


def _mla_attention(qt, km, vt):
    B, H, _, S = qt.shape
    tq, tk = MLA_Q_COLS, MLA_K_ROWS
    assert tq % tk == 0
    return pl.pallas_call(
        _mla_kernel,
        grid=(B, S // tq),
        in_specs=[pl.BlockSpec((None, H, LANES, tq), lambda b, n: (b, 0, 0, n)),
                  pl.BlockSpec((None, H, S, LANES), lambda b, n: (b, 0, 0, 0)),
                  pl.BlockSpec((None, H, S // tk, MLA_V_DIM, tk), lambda b, n: (b, 0, 0, 0, 0))],
        out_specs=pl.BlockSpec((None, tq, H * MLA_V_DIM), lambda b, n: (b, n, 0)),
        out_shape=jax.ShapeDtypeStruct((B, S, H * MLA_V_DIM), BF16),
        scratch_shapes=[pltpu.VMEM((H, 1, tq), F32),
                        pltpu.VMEM((H, 1, tq), F32),
                        pltpu.VMEM((H, MLA_V_DIM, tq), F32)],
        compiler_params=pltpu.CompilerParams(
            dimension_semantics=("arbitrary", "arbitrary"), vmem_limit_bytes=VMEM_LIMIT_BYTES),
        name="mla_attention",
    )(qt, km, vt)


def _post_kernel(x_ref, oa_ref, ob_ref, g1_ref, wg_ref, woa_ref, wob_ref, wout_ref,
                 g2_ref, wgate_ref, wup_ref, wdown_ref, g3_ref, out_ref, *, final_norm):
    D = x_ref.shape[-1]
    x = x_ref[...]
    h = _rmsnorm_f32(x, g1_ref[...]).astype(BF16)
    gates = jnp.dot(h, wg_ref[...], preferred_element_type=F32)
    ya = jnp.dot(oa_ref[...], woa_ref[...], preferred_element_type=F32)
    yb = jnp.dot(ob_ref[...], wob_ref[...], preferred_element_type=F32)
    y = jax.nn.sigmoid(gates[:, :D]) * ya + jax.nn.sigmoid(gates[:, D:]) * yb
    x1 = x + jnp.dot(y.astype(BF16), wout_ref[...], preferred_element_type=F32)

    h2 = _rmsnorm_f32(x1, g2_ref[...]).astype(BF16)
    hg = jnp.dot(h2, wgate_ref[...], preferred_element_type=F32)
    hu = jnp.dot(h2, wup_ref[...], preferred_element_type=F32)
    a = (hg * jax.nn.sigmoid(hg) * hu).astype(BF16)
    x2 = x1 + jnp.dot(a, wdown_ref[...], preferred_element_type=F32)
    out_ref[...] = _rmsnorm_f32(x2, g3_ref[...]) if final_norm else x2


def _post(x, oa, ob, g1, wg, woa, wob, wout, g2, wgate, wup, wdown, g3, final_norm):
    B, S, D = x.shape
    rows = POST_ROWS
    tok = lambda w: pl.BlockSpec((None, rows, w), lambda b, i: (b, i, 0))
    const = lambda a: pl.BlockSpec(a.shape, lambda b, i: (0,) * a.ndim, pipeline_mode=pl.Buffered(1))
    return pl.pallas_call(
        functools.partial(_post_kernel, final_norm=final_norm),
        grid=(B, S // rows),
        in_specs=[tok(D), tok(oa.shape[-1]), tok(ob.shape[-1])]
                 + [const(a) for a in (g1, wg, woa, wob, wout, g2, wgate, wup, wdown, g3)],
        out_specs=tok(D),
        out_shape=jax.ShapeDtypeStruct((B, S, D), F32),
        compiler_params=pltpu.CompilerParams(
            dimension_semantics=("arbitrary", "arbitrary"), vmem_limit_bytes=VMEM_LIMIT_BYTES),
        name="post_attention",
    )(x, oa, ob, g1, wg, woa, wob, wout, g2, wgate, wup, wdown, g3)


def _rope_angles(seq, dim):
    inv = ROPE_THETA ** (-jnp.arange(0, dim, 2, dtype=F32) / dim)
    ang = jnp.arange(seq, dtype=F32)[:, None] * inv[None, :]
    return jnp.cos(ang), jnp.sin(ang)


def _rope_table(seq, dim, group_offsets):
    half = dim // 2
    cos, sin = _rope_angles(seq, dim)
    one = jnp.ones((seq, 1), F32)
    zero = jnp.zeros((seq, 1), F32)
    c, s1, s2 = [], [], []
    lane = 0
    for o in group_offsets:
        pad = o - lane
        c += [jnp.tile(one, (1, pad)), cos, cos]
        s1 += [jnp.tile(zero, (1, pad)), -sin, jnp.tile(zero, (1, half))]
        s2 += [jnp.tile(zero, (1, pad)), jnp.tile(zero, (1, half)), sin]
        lane = o + dim
    pad = LANES - lane
    c.append(jnp.tile(one, (1, pad)))
    s1.append(jnp.tile(zero, (1, pad)))
    s2.append(jnp.tile(zero, (1, pad)))
    return jnp.stack([jnp.concatenate(t, axis=1) for t in (c, s1, s2)])


def kernel(x, mix_norm_g, w_in, swa_sinks, q_norm_g, w_uq, kv_norm_g, w_ukv, w_o_swa, w_o_mla,
           w_out, ffn_norm_g, w_gate, w_up, w_down, final_norm_g):
    B, S, D = x.shape
    depth = w_in.shape[0]
    assert S % SWA_Q_ROWS == 0 and S % PROJ_ROWS == 0 and S % POST_ROWS == 0 and S % MLA_Q_COLS == 0
    assert PROJ_ROWS % MLA_K_ROWS == 0

    swa_tab = _rope_table(S, HEAD_DIM, (0, HEAD_DIM))
    mk_tab = _rope_table(S, MLA_ROPE_DIM, (MLA_NOPE_DIM,))
    cos_m, sin_m = _rope_angles(S, MLA_ROPE_DIM)
    mqt_tab = jnp.stack([cos_m.T, sin_m.T]) * MLA_Q_SCALE
    row = lambda v: v.reshape(1, -1)

    for l in range(depth):
        wi = w_in[l]
        wq = wi[:, :512].reshape(D, SWA_HEADS, HEAD_DIM)[:, jnp.array(SWA_Q_ORDER)].reshape(D, 512) * (HEAD_DIM ** -0.5)
        wkr = jnp.pad(wi[:, 1408:1440], ((0, 0), (MLA_NOPE_DIM, LANES - MLA_QK_DIM)))
        win = jnp.concatenate([wq, wi[:, 512:1408], wkr], axis=1).astype(BF16)
        wg = wi[:, 1440:].astype(BF16)
        wuqt = jnp.pad(w_uq[l].reshape(Q_LORA_RANK, MLA_HEADS, MLA_QK_DIM),
                       ((0, 0), (0, 0), (0, LANES - MLA_QK_DIM))).reshape(Q_LORA_RANK, -1).T.astype(BF16)
        wkv = w_ukv[l].reshape(KV_LORA_RANK, MLA_HEADS, MLA_NOPE_DIM + MLA_V_DIM)
        wuk = jnp.pad(wkv[:, :, :MLA_NOPE_DIM],
                      ((0, 0), (0, 0), (0, LANES - MLA_NOPE_DIM))).reshape(KV_LORA_RANK, -1).astype(BF16)
        wuvt = wkv[:, :, MLA_NOPE_DIM:].reshape(KV_LORA_RANK, -1).T.astype(BF16)
        woa = w_o_swa[l].reshape(SWA_HEADS, HEAD_DIM, D)[jnp.array(SWA_Q_ORDER)].reshape(-1, D).astype(BF16)

        qa, ka, va, qt, km, vt = _in_proj(x, row(mix_norm_g[l]), win, row(q_norm_g[l]), wuqt,
                                          row(kv_norm_g[l]), wuk, wuvt, swa_tab, mk_tab, mqt_tab)
        oa = _swa_attention(swa_sinks[l], qa, ka, va)
        ob = _mla_attention(qt, km, vt)
        x = _post(x, oa, ob, row(mix_norm_g[l]), wg, woa, w_o_mla[l].astype(BF16), w_out[l].astype(BF16),
                  row(ffn_norm_g[l]), w_gate[l].astype(BF16), w_up[l].astype(BF16),
                  w_down[l].astype(BF16), row(final_norm_g), final_norm=(l == depth - 1))
    return x
```

```python
import functools
import math

import jax
import jax.numpy as jnp
from jax import lax
from jax.experimental import pallas as pl
from jax.experimental.pallas import tpu as pltpu

EPS = 1e-6
ROPE_THETA = 10000.0
BLOCK = 128
HEAD_DIM = 64
SWA_HEADS = 8
SWA_KV_HEADS = 2
SWA_GROUP = SWA_HEADS // SWA_KV_HEADS
MLA_HEADS = 8
MLA_NOPE_DIM = 64
MLA_ROPE_DIM = 32
MLA_V_DIM = 64
MLA_QK_DIM = MLA_NOPE_DIM + MLA_ROPE_DIM
Q_LORA_RANK = 384
KV_LORA_RANK = 256

LANES = 128
V7X_VMEM_BYTES = 64 * 1024 * 1024
VMEM_LIMIT_BYTES = 56 * 1024 * 1024

PROJ_ROWS = 1024
SWA_Q_ROWS = 512
MLA_Q_COLS = 512
MLA_K_ROWS = 256
POST_ROWS = 1024
POST_SUB_ROWS = 256
POST_INTERLEAVE = 2
MLA_LOOKAHEAD = 2

MLA_HEAD_PAD = LANES
MLA_ROPE_HALF = MLA_ROPE_DIM // 2
MLA_V_ROWS = MLA_V_DIM + 16
SWA_Q_ORDER = (0, 4, 1, 5, 2, 6, 3, 7)
MLA_Q_SCALE = (MLA_QK_DIM ** -0.5) * math.log2(math.e)

F32 = jnp.float32
BF16 = jnp.bfloat16
NT_DIMS = (((1,), (1,)), ((), ()))


def _rmsnorm_f32(x, g):
    return x * lax.rsqrt(jnp.mean(x * x, axis=-1, keepdims=True) + EPS) * g


def _rope_lane_group(x, tab, half):
    up = pltpu.roll(x, LANES - half, axis=1)
    down = pltpu.roll(x, half, axis=1)
    return x * tab[0] + up * tab[1] + down * tab[2]


def _in_proj_kernel(x_ref, g_ref, win_ref, qg_ref, wuqt_ref, kvg_ref, wuk_ref, wuvt_ref,
                    swa_tab, mk_tab, mqt_tab,
                    qa_ref, ka_ref, va_ref, qt_ref, km_ref, vt_ref):
    n_lat = Q_LORA_RANK + KV_LORA_RANK
    n_first = n_lat + LANES
    o_q, o_v, o_kr = 0, 512, 640
    sub = MLA_K_ROWS
    ones = jnp.ones((MLA_V_ROWS - MLA_V_DIM, sub), F32)
    r0, r1, r2 = MLA_NOPE_DIM, MLA_NOPE_DIM + MLA_ROPE_HALF, MLA_QK_DIM

    for t in range(x_ref.shape[0] // sub):
        rs = slice(t * sub, (t + 1) * sub)
        h = _rmsnorm_f32(x_ref[rs, :], g_ref[...]).astype(BF16)
        lat = jnp.dot(h, win_ref[:, :n_first], preferred_element_type=F32)
        p = jnp.dot(h, win_ref[:, n_first:], preferred_element_type=F32)
        cq = _rmsnorm_f32(lat[:, :Q_LORA_RANK], qg_ref[...]).astype(BF16)
        ckv = _rmsnorm_f32(lat[:, Q_LORA_RANK:n_lat], kvg_ref[...]).astype(BF16)
        qt = lax.dot_general(wuqt_ref[...], cq, NT_DIMS, preferred_element_type=F32)
        kb = jnp.dot(ckv, wuk_ref[...], preferred_element_type=F32)
        vt = lax.dot_general(wuvt_ref[...], ckv, NT_DIMS, preferred_element_type=F32)

        swa = (swa_tab[0, rs, :], swa_tab[1, rs, :], swa_tab[2, rs, :])
        for c in range(4):
            qa_ref[rs, c * LANES:(c + 1) * LANES] = _rope_lane_group(
                p[:, o_q + c * LANES:o_q + (c + 1) * LANES], swa, HEAD_DIM // 2).astype(BF16)
        ka_ref[rs, :] = _rope_lane_group(lat[:, n_lat:], swa, HEAD_DIM // 2).astype(BF16)
        va_ref[rs, :] = p[:, o_v:o_v + LANES].astype(BF16)

        cos_q, sin_q = mqt_tab[0, :, rs], mqt_tab[1, :, rs]
        for hd in range(MLA_HEADS):
            qh = qt[hd * LANES:(hd + 1) * LANES]
            x1, x2 = qh[r0:r1], qh[r1:r2]
            qt_ref[hd, :, rs] = jnp.concatenate(
                [qh[:r0] * MLA_Q_SCALE, x1 * cos_q - x2 * sin_q, x2 * cos_q + x1 * sin_q, qh[r2:]],
                axis=0).astype(BF16)

        mk = (mk_tab[0, rs, :], mk_tab[1, rs, :], mk_tab[2, rs, :])
        kr = _rope_lane_group(p[:, o_kr:o_kr + LANES], mk, MLA_ROPE_HALF)
        for hd in range(MLA_HEADS):
            km_ref[hd, rs, :] = (kb[:, hd * LANES:(hd + 1) * LANES] + kr).astype(BF16)
            vt_ref[hd, t] = jnp.concatenate(
                [vt[hd * MLA_V_DIM:(hd + 1) * MLA_V_DIM], ones], axis=0).astype(BF16)


def _in_proj(x, g, win, qg, wuqt, kvg, wuk, wuvt, swa_tab, mk_tab, mqt_tab):
    B, S, D = x.shape
    rows = PROJ_ROWS
    ns = S // rows
    kc = rows // MLA_K_ROWS
    H = MLA_HEADS
    tok = lambda w: pl.BlockSpec((None, rows, w), lambda si, b: (b, si, 0))
    const = lambda a: pl.BlockSpec(a.shape, lambda si, b: (0,) * a.ndim)
    tab = pl.BlockSpec((3, rows, LANES), lambda si, b: (0, si, 0))
    tabt = pl.BlockSpec((2, MLA_ROPE_HALF, rows), lambda si, b: (0, 0, si))
    return pl.pallas_call(
        _in_proj_kernel,
        grid=(ns, B),
        in_specs=[tok(D), const(g), const(win), const(qg), const(wuqt), const(kvg), const(wuk),
                  const(wuvt), tab, tab, tabt],
        out_specs=[tok(512), tok(LANES), tok(LANES),
                   pl.BlockSpec((None, H, LANES, rows), lambda si, b: (b, 0, 0, si)),
                   pl.BlockSpec((None, H, rows, LANES), lambda si, b: (b, 0, si, 0)),
                   pl.BlockSpec((None, H, kc, MLA_V_ROWS, MLA_K_ROWS), lambda si, b: (b, 0, si, 0, 0))],
        out_shape=[jax.ShapeDtypeStruct((B, S, 512), BF16),
                   jax.ShapeDtypeStruct((B, S, LANES), BF16),
                   jax.ShapeDtypeStruct((B, S, LANES), BF16),
                   jax.ShapeDtypeStruct((B, H, LANES, S), BF16),
                   jax.ShapeDtypeStruct((B, H, S, LANES), BF16),
                   jax.ShapeDtypeStruct((B, H, S // MLA_K_ROWS, MLA_V_ROWS, MLA_K_ROWS), BF16)],
        compiler_params=pltpu.CompilerParams(
            dimension_semantics=("arbitrary", "arbitrary"), vmem_limit_bytes=VMEM_LIMIT_BYTES),
        name="in_proj",
    )(x, g, win, qg, wuqt, kvg, wuk, wuvt, swa_tab, mk_tab, mqt_tab)


def _swa_kernel(sink_ref, q_ref, k_ref, v_ref, o_ref):
    n = pl.program_id(1)
    lane = lax.broadcasted_iota(jnp.int32, (BLOCK, LANES), 1)
    low = lane < HEAD_DIM
    qi = lax.broadcasted_iota(jnp.int32, (BLOCK, 2 * BLOCK), 0)
    kj = lax.broadcasted_iota(jnp.int32, (BLOCK, 2 * BLOCK), 1)
    diff = qi - kj + BLOCK
    band = (diff >= 0) & (diff < BLOCK)

    zero = jnp.zeros((BLOCK, LANES), BF16)
    n_blk = SWA_Q_ROWS // BLOCK

    def window(ref, blk):
        prev0 = pl.multiple_of(jnp.maximum(blk - 1, 0) * BLOCK, BLOCK)
        cur0 = pl.multiple_of(blk * BLOCK, BLOCK)
        return jnp.concatenate([ref[pl.ds(prev0, BLOCK), :], ref[pl.ds(cur0, BLOCK), :]], axis=0)

    def scores(j):
        qs = []
        for c in range(4):
            qc = q_ref[j * BLOCK:(j + 1) * BLOCK, c * LANES:(c + 1) * LANES]
            qs.append(jnp.where(low, qc, zero))
            qs.append(jnp.where(low, zero, qc))
        q8 = jnp.concatenate(qs, axis=0)
        return lax.dot_general(q8, window(k_ref, n * n_blk + j), NT_DIMS, preferred_element_type=F32)

    def softmax(j, s):
        blk = n * n_blk + j
        mask = band & ((kj >= BLOCK) | (blk > 0))
        ps, inv = [], []
        for i in range(2 * 4):
            head = (i // 2) + 4 * (i % 2)
            sink = sink_ref[head]
            si = jnp.where(mask, s[i * BLOCK:(i + 1) * BLOCK], -jnp.inf)
            m = jnp.maximum(jnp.max(si, axis=-1, keepdims=True), sink)
            pi = jnp.exp(si - m)
            denom = jnp.sum(pi, axis=-1, keepdims=True) + jnp.exp(sink - m)
            ps.append(pi.astype(BF16))
            inv.append(1.0 / denom)
        return jnp.concatenate(ps, axis=0), inv

    def weighted_values(j, p8, inv):
        o8 = jnp.dot(p8, window(v_ref, n * n_blk + j), preferred_element_type=F32)
        for c in range(4):
            oa = o8[(2 * c) * BLOCK:(2 * c + 1) * BLOCK] * inv[2 * c]
            ob = o8[(2 * c + 1) * BLOCK:(2 * c + 2) * BLOCK] * inv[2 * c + 1]
            o_ref[j * BLOCK:(j + 1) * BLOCK, c * LANES:(c + 1) * LANES] = jnp.where(low, oa, ob).astype(BF16)

    for j in range(n_blk):
        weighted_values(j, *softmax(j, scores(j)))


def _swa_attention(sinks, qa, ka, va):
    B, S, _ = qa.shape
    rows = SWA_Q_ROWS
    return pl.pallas_call(
        _swa_kernel,
        grid=(B, S // rows),
        in_specs=[pl.BlockSpec(memory_space=pltpu.SMEM),
                  pl.BlockSpec((None, rows, 512), lambda b, n: (b, n, 0)),
                  pl.BlockSpec((None, S, LANES), lambda b, n: (b, 0, 0)),
                  pl.BlockSpec((None, S, LANES), lambda b, n: (b, 0, 0))],
        out_specs=pl.BlockSpec((None, rows, 512), lambda b, n: (b, n, 0)),
        out_shape=jax.ShapeDtypeStruct((B, S, 512), BF16),
        compiler_params=pltpu.CompilerParams(
            dimension_semantics=("arbitrary", "arbitrary"), vmem_limit_bytes=VMEM_LIMIT_BYTES),
        name="swa_attention",
    )(sinks, qa, ka, va)


def _mla_kernel(qt_ref, k_ref, vt_ref, o_ref, m_ref, acc_ref):
    n = pl.program_id(1)
    tq, tk = MLA_Q_COLS, MLA_K_ROWS
    per_tile = tq // tk

    m_ref[...] = jnp.full(m_ref.shape, -jnp.inf, F32)
    acc_ref[...] = jnp.zeros(acc_ref.shape, F32)

    def scores(unit):
        j, h, c0, masked = unit
        k0 = pl.multiple_of(j * tk, tk)
        s = jnp.dot(k_ref[h, pl.ds(k0, tk), :], qt_ref[h, :, c0:], preferred_element_type=F32)
        if masked:
            key = lax.broadcasted_iota(jnp.int32, s.shape, 0)
            qry = lax.broadcasted_iota(jnp.int32, s.shape, 1)
            s = jnp.where(key + (j * tk - n * tq - c0) <= qry, s, -jnp.inf)
        return s

    def weighted_values(unit, p, alpha):
        j, h, c0, _ = unit
        acc_ref[h, :, c0:] = alpha * acc_ref[h, :, c0:] + jnp.dot(vt_ref[h, j], p, preferred_element_type=F32)

    def run(units):
        ahead = {i: scores(units[i]) for i in range(min(MLA_LOOKAHEAD, len(units)))}
        pending = None
        for i, unit in enumerate(units):
            if i + MLA_LOOKAHEAD < len(units):
                ahead[i + MLA_LOOKAHEAD] = scores(units[i + MLA_LOOKAHEAD])
            s = ahead.pop(i)
            _, h, c0, _ = unit
            m_prev = m_ref[h, :, c0:]
            m_new = jnp.maximum(m_prev, jnp.max(s, axis=0, keepdims=True))
            alpha = jnp.exp2(m_prev - m_new)
            p = jnp.exp2(s - m_new).astype(BF16)
            m_ref[h, :, c0:] = m_new
            if pending is not None:
                weighted_values(*pending)
            pending = (unit, p, alpha)
        weighted_values(*pending)

    def body(i, carry):
        run([(i * per_tile + d, h, 0, False) for d in range(per_tile) for h in range(MLA_HEADS)])
        return carry

    lax.fori_loop(0, n, body, 0)
    run([(n * per_tile + d, h, d * tk, True) for d in range(per_tile) for h in range(MLA_HEADS)])

    ot = jnp.concatenate([acc_ref[h, :MLA_V_DIM] / acc_ref[h, MLA_V_DIM:MLA_V_DIM + 1]
                          for h in range(MLA_HEADS)], axis=0)
    o_ref[...] = ot.T.astype(BF16)


def _mla_attention(qt, km, vt):
    B, H, _, S = qt.shape
    tq, tk = MLA_Q_COLS, MLA_K_ROWS
    assert tq % tk == 0
    return pl.pallas_call(
        _mla_kernel,
        grid=(B, S // tq),
        in_specs=[pl.BlockSpec((None, H, LANES, tq), lambda b, n: (b, 0, 0, n)),
                  pl.BlockSpec((None, H, S, LANES), lambda b, n: (b, 0, 0, 0)),
                  pl.BlockSpec((None, H, S // tk, MLA_V_ROWS, tk), lambda b, n: (b, 0, 0, 0, 0))],
        out_specs=pl.BlockSpec((None, tq, H * MLA_V_DIM), lambda b, n: (b, n, 0)),
        out_shape=jax.ShapeDtypeStruct((B, S, H * MLA_V_DIM), BF16),
        scratch_shapes=[pltpu.VMEM((H, 1, tq), F32),
                        pltpu.VMEM((H, MLA_V_ROWS, tq), F32)],
        compiler_params=pltpu.CompilerParams(
            dimension_semantics=("arbitrary", "arbitrary"), vmem_limit_bytes=VMEM_LIMIT_BYTES),
        name="mla_attention",
    )(qt, km, vt)


def _post_kernel(x_ref, oa_ref, ob_ref, g1_ref, wg_ref, woa_ref, wob_ref, wout_ref,
                 g2_ref, wgate_ref, wup_ref, wdown_ref, g3_ref, out_ref, *, final_norm):
    D = x_ref.shape[-1]
    dot = functools.partial(jnp.dot, preferred_element_type=F32)
    n_sub = x_ref.shape[0] // POST_SUB_ROWS
    for t0 in range(0, n_sub, POST_INTERLEAVE):
        tiles = [slice((t0 + i) * POST_SUB_ROWS, (t0 + i + 1) * POST_SUB_ROWS) for i in range(POST_INTERLEAVE)]
        xs = [x_ref[rs, :] for rs in tiles]
        hs = [_rmsnorm_f32(x, g1_ref[...]).astype(BF16) for x in xs]
        gates = [dot(h, wg_ref[...]) for h in hs]
        yas = [dot(oa_ref[rs, :], woa_ref[...]) for rs in tiles]
        ybs = [dot(ob_ref[rs, :], wob_ref[...]) for rs in tiles]
        ys = [(jax.nn.sigmoid(g[:, :D]) * ya + jax.nn.sigmoid(g[:, D:]) * yb).astype(BF16)
              for g, ya, yb in zip(gates, yas, ybs)]
        x1s = [x + dot(y, wout_ref[...]) for x, y in zip(xs, ys)]

        h2s = [_rmsnorm_f32(x1, g2_ref[...]).astype(BF16) for x1 in x1s]
        hgs = [dot(h2, wgate_ref[...]) for h2 in h2s]
        hus = [dot(h2, wup_ref[...]) for h2 in h2s]
        acts = [(hg * jax.nn.sigmoid(hg) * hu).astype(BF16) for hg, hu in zip(hgs, hus)]
        x2s = [x1 + dot(a, wdown_ref[...]) for x1, a in zip(x1s, acts)]
        for rs, x2 in zip(tiles, x2s):
            out_ref[rs, :] = _rmsnorm_f32(x2, g3_ref[...]) if final_norm else x2


def _post(x, oa, ob, g1, wg, woa, wob, wout, g2, wgate, wup, wdown, g3, final_norm):
    B, S, D = x.shape
    rows = POST_ROWS
    tok = lambda w: pl.BlockSpec((None, rows, w), lambda b, i: (b, i, 0))
    const = lambda a: pl.BlockSpec(a.shape, lambda b, i: (0,) * a.ndim, pipeline_mode=pl.Buffered(1))
    return pl.pallas_call(
        functools.partial(_post_kernel, final_norm=final_norm),
        grid=(B, S // rows),
        in_specs=[tok(D), tok(oa.shape[-1]), tok(ob.shape[-1])]
                 + [const(a) for a in (g1, wg, woa, wob, wout, g2, wgate, wup, wdown, g3)],
        out_specs=tok(D),
        out_shape=jax.ShapeDtypeStruct((B, S, D), F32),
        compiler_params=pltpu.CompilerParams(
            dimension_semantics=("arbitrary", "arbitrary"), vmem_limit_bytes=VMEM_LIMIT_BYTES),
        name="post_attention",
    )(x, oa, ob, g1, wg, woa, wob, wout, g2, wgate, wup, wdown, g3)


def _rope_angles(seq, dim):
    inv = ROPE_THETA ** (-jnp.arange(0, dim, 2, dtype=F32) / dim)
    ang = jnp.arange(seq, dtype=F32)[:, None] * inv[None, :]
    return jnp.cos(ang), jnp.sin(ang)


def _rope_table(seq, dim, group_offsets):
    half = dim // 2
    cos, sin = _rope_angles(seq, dim)
    one = jnp.ones((seq, 1), F32)
    zero = jnp.zeros((seq, 1), F32)
    c, s1, s2 = [], [], []
    lane = 0
    for o in group_offsets:
        pad = o - lane
        c += [jnp.tile(one, (1, pad)), cos, cos]
        s1 += [jnp.tile(zero, (1, pad)), -sin, jnp.tile(zero, (1, half))]
        s2 += [jnp.tile(zero, (1, pad)), jnp.tile(zero, (1, half)), sin]
        lane = o + dim
    pad = LANES - lane
    c.append(jnp.tile(one, (1, pad)))
    s1.append(jnp.tile(zero, (1, pad)))
    s2.append(jnp.tile(zero, (1, pad)))
    return jnp.stack([jnp.concatenate(t, axis=1) for t in (c, s1, s2)])


def kernel(x, mix_norm_g, w_in, swa_sinks, q_norm_g, w_uq, kv_norm_g, w_ukv, w_o_swa, w_o_mla,
           w_out, ffn_norm_g, w_gate, w_up, w_down, final_norm_g):
    B, S, D = x.shape
    depth = w_in.shape[0]
    assert S % SWA_Q_ROWS == 0 and S % PROJ_ROWS == 0 and S % POST_ROWS == 0 and S % MLA_Q_COLS == 0
    assert PROJ_ROWS % MLA_K_ROWS == 0

    swa_tab = _rope_table(S, HEAD_DIM, (0, HEAD_DIM))
    mk_tab = _rope_table(S, MLA_ROPE_DIM, (MLA_NOPE_DIM,))
    cos_m, sin_m = _rope_angles(S, MLA_ROPE_DIM)
    mqt_tab = jnp.stack([cos_m.T, sin_m.T]) * MLA_Q_SCALE
    row = lambda v: v.reshape(1, -1)

    for l in range(depth):
        wi = w_in[l]
        wq = wi[:, :512].reshape(D, SWA_HEADS, HEAD_DIM)[:, jnp.array(SWA_Q_ORDER)].reshape(D, 512) * (HEAD_DIM ** -0.5)
        wkr = jnp.pad(wi[:, 1408:1440], ((0, 0), (MLA_NOPE_DIM, LANES - MLA_QK_DIM)))
        win = jnp.concatenate([wi[:, 768:1408], wi[:, 512:640], wq, wi[:, 640:768], wkr],
                              axis=1).astype(BF16)
        wg = wi[:, 1440:].astype(BF16)
        wuqt = jnp.pad(w_uq[l].reshape(Q_LORA_RANK, MLA_HEADS, MLA_QK_DIM),
                       ((0, 0), (0, 0), (0, LANES - MLA_QK_DIM))).reshape(Q_LORA_RANK, -1).T.astype(BF16)
        wkv = w_ukv[l].reshape(KV_LORA_RANK, MLA_HEADS, MLA_NOPE_DIM + MLA_V_DIM)
        wuk = jnp.pad(wkv[:, :, :MLA_NOPE_DIM],
                      ((0, 0), (0, 0), (0, LANES - MLA_NOPE_DIM))).reshape(KV_LORA_RANK, -1).astype(BF16)
        wuvt = wkv[:, :, MLA_NOPE_DIM:].reshape(KV_LORA_RANK, -1).T.astype(BF16)
        woa = w_o_swa[l].reshape(SWA_HEADS, HEAD_DIM, D)[jnp.array(SWA_Q_ORDER)].reshape(-1, D).astype(BF16)

        qa, ka, va, qt, km, vt = _in_proj(x, row(mix_norm_g[l]), win, row(q_norm_g[l]), wuqt,
                                          row(kv_norm_g[l]), wuk, wuvt, swa_tab, mk_tab, mqt_tab)
        oa = _swa_attention(swa_sinks[l], qa, ka, va)
        ob = _mla_attention(qt, km, vt)
        x = _post(x, oa, ob, row(mix_norm_g[l]), wg, woa, w_o_mla[l].astype(BF16), w_out[l].astype(BF16),
                  row(ffn_norm_g[l]), w_gate[l].astype(BF16), w_up[l].astype(BF16),
                  w_down[l].astype(BF16), row(final_norm_g), final_norm=(l == depth - 1))
    return x
```

```python
import functools
import math

import jax
import jax.numpy as jnp
from jax import lax
from jax.experimental import pallas as pl
from jax.experimental.pallas import tpu as pltpu

EPS = 1e-6
ROPE_THETA = 10000.0
BLOCK = 128
HEAD_DIM = 64
SWA_HEADS = 8
SWA_KV_HEADS = 2
SWA_GROUP = SWA_HEADS // SWA_KV_HEADS
MLA_HEADS = 8
MLA_NOPE_DIM = 64
MLA_ROPE_DIM = 32
MLA_V_DIM = 64
MLA_QK_DIM = MLA_NOPE_DIM + MLA_ROPE_DIM
Q_LORA_RANK = 384
KV_LORA_RANK = 256

LANES = 128
V7X_VMEM_BYTES = 64 * 1024 * 1024
VMEM_LIMIT_BYTES = 56 * 1024 * 1024

PROJ_ROWS = 1024
SWA_Q_ROWS = 1024
MLA_Q_COLS = 512
MLA_K_ROWS = 256
POST_ROWS = 1024
POST_SUB_ROWS = 256
POST_INTERLEAVE = 2
MLA_LOOKAHEAD = 2
SWA_LOOKAHEAD = 2

MLA_HEAD_PAD = LANES
MLA_ROPE_HALF = MLA_ROPE_DIM // 2
MLA_V_ROWS = MLA_V_DIM + 16
SWA_V_ROWS = HEAD_DIM + 16
LOG2_E = math.log2(math.e)
MLA_Q_SCALE = (MLA_QK_DIM ** -0.5) * LOG2_E
SWA_Q_SCALE = (HEAD_DIM ** -0.5) * LOG2_E

F32 = jnp.float32
BF16 = jnp.bfloat16
NT_DIMS = (((1,), (1,)), ((), ()))


def _rmsnorm_f32(x, g):
    return x * lax.rsqrt(jnp.mean(x * x, axis=-1, keepdims=True) + EPS) * g


def _software_pipeline(units, lookahead, first, middle, last):
    ahead = {i: first(units[i]) for i in range(min(lookahead, len(units)))}
    pending = None
    for i, unit in enumerate(units):
        if i + lookahead < len(units):
            ahead[i + lookahead] = first(units[i + lookahead])
        mid = middle(unit, ahead.pop(i))
        if pending is not None:
            last(pending[0], *pending[1])
        pending = (unit, mid)
    last(pending[0], *pending[1])


def _rope_lane_group(x, tab, half):
    up = pltpu.roll(x, LANES - half, axis=1)
    down = pltpu.roll(x, half, axis=1)
    return x * tab[0] + up * tab[1] + down * tab[2]


def _in_proj_kernel(x_ref, g_ref, win_ref, wqt_ref, qg_ref, wuqt_ref, kvg_ref, wuk_ref, wuvt_ref,
                    swa_tab, swat_tab, mk_tab, mqt_tab,
                    qat_ref, ka_ref, vat_ref, qt_ref, km_ref, vt_ref):
    n_lat = Q_LORA_RANK + KV_LORA_RANK
    n_first = n_lat + LANES
    o_v, o_kr = 0, LANES
    sub = MLA_K_ROWS
    ones = jnp.ones((MLA_V_ROWS - MLA_V_DIM, sub), F32)
    r0, r1, r2 = MLA_NOPE_DIM, MLA_NOPE_DIM + MLA_ROPE_HALF, MLA_QK_DIM
    half = HEAD_DIM // 2

    for t in range(x_ref.shape[0] // sub):
        rs = slice(t * sub, (t + 1) * sub)
        h = _rmsnorm_f32(x_ref[rs, :], g_ref[...]).astype(BF16)
        lat = jnp.dot(h, win_ref[:, :n_first], preferred_element_type=F32)
        qat = lax.dot_general(wqt_ref[...], h, NT_DIMS, preferred_element_type=F32)
        p = jnp.dot(h, win_ref[:, n_first:], preferred_element_type=F32)
        cq = _rmsnorm_f32(lat[:, :Q_LORA_RANK], qg_ref[...]).astype(BF16)
        ckv = _rmsnorm_f32(lat[:, Q_LORA_RANK:n_lat], kvg_ref[...]).astype(BF16)
        qt = lax.dot_general(wuqt_ref[...], cq, NT_DIMS, preferred_element_type=F32)
        kb = jnp.dot(ckv, wuk_ref[...], preferred_element_type=F32)
        vt = lax.dot_general(wuvt_ref[...], ckv, NT_DIMS, preferred_element_type=F32)

        cos_a, sin_a = swat_tab[0, :, rs], swat_tab[1, :, rs]
        for hd in range(SWA_HEADS):
            x1, x2 = qat[hd * HEAD_DIM:hd * HEAD_DIM + half], qat[hd * HEAD_DIM + half:(hd + 1) * HEAD_DIM]
            qat_ref[hd, :, rs] = jnp.concatenate(
                [x1 * cos_a - x2 * sin_a, x2 * cos_a + x1 * sin_a], axis=0).astype(BF16)
        swa = (swa_tab[0, rs, :], swa_tab[1, rs, :], swa_tab[2, rs, :])
        ka_ref[rs, :] = _rope_lane_group(lat[:, n_lat:], swa, half).astype(BF16)
        vat = p[:, o_v:o_v + LANES].T
        for g in range(SWA_KV_HEADS):
            for i in range(sub // BLOCK):
                vat_ref[g, t * (sub // BLOCK) + i] = jnp.concatenate(
                    [vat[g * HEAD_DIM:(g + 1) * HEAD_DIM, i * BLOCK:(i + 1) * BLOCK], ones[:, :BLOCK]],
                    axis=0).astype(BF16)

        cos_q, sin_q = mqt_tab[0, :, rs], mqt_tab[1, :, rs]
        for hd in range(MLA_HEADS):
            qh = qt[hd * LANES:(hd + 1) * LANES]
            x1, x2 = qh[r0:r1], qh[r1:r2]
            qt_ref[hd, :, rs] = jnp.concatenate(
                [qh[:r0] * MLA_Q_SCALE, x1 * cos_q - x2 * sin_q, x2 * cos_q + x1 * sin_q, qh[r2:]],
                axis=0).astype(BF16)

        mk = (mk_tab[0, rs, :], mk_tab[1, rs, :], mk_tab[2, rs, :])
        kr = _rope_lane_group(p[:, o_kr:o_kr + LANES], mk, MLA_ROPE_HALF)
        for hd in range(MLA_HEADS):
            km_ref[hd, rs, :] = (kb[:, hd * LANES:(hd + 1) * LANES] + kr).astype(BF16)
            vt_ref[hd, t] = jnp.concatenate(
                [vt[hd * MLA_V_DIM:(hd + 1) * MLA_V_DIM], ones], axis=0).astype(BF16)


def _in_proj(x, g, win, wqt, qg, wuqt, kvg, wuk, wuvt, swa_tab, swat_tab, mk_tab, mqt_tab):
    B, S, D = x.shape
    rows = PROJ_ROWS
    ns = S // rows
    kc = rows // MLA_K_ROWS
    H = MLA_HEADS
    G = SWA_KV_HEADS
    tok = lambda w: pl.BlockSpec((None, rows, w), lambda si, b: (b, si, 0))
    const = lambda a: pl.BlockSpec(a.shape, lambda si, b: (0,) * a.ndim)
    tab = pl.BlockSpec((3, rows, LANES), lambda si, b: (0, si, 0))
    tabt = lambda half: pl.BlockSpec((2, half, rows), lambda si, b: (0, 0, si))
    return pl.pallas_call(
        _in_proj_kernel,
        grid=(ns, B),
        in_specs=[tok(D), const(g), const(win), const(wqt), const(qg), const(wuqt), const(kvg), const(wuk),
                  const(wuvt), tab, tabt(HEAD_DIM // 2), tab, tabt(MLA_ROPE_HALF)],
        out_specs=[pl.BlockSpec((None, SWA_HEADS, HEAD_DIM, rows), lambda si, b: (b, 0, 0, si)),
                   tok(LANES),
                   pl.BlockSpec((None, G, rows // BLOCK, SWA_V_ROWS, BLOCK), lambda si, b: (b, 0, si, 0, 0)),
                   pl.BlockSpec((None, H, LANES, rows), lambda si, b: (b, 0, 0, si)),
                   pl.BlockSpec((None, H, rows, LANES), lambda si, b: (b, 0, si, 0)),
                   pl.BlockSpec((None, H, kc, MLA_V_ROWS, MLA_K_ROWS), lambda si, b: (b, 0, si, 0, 0))],
        out_shape=[jax.ShapeDtypeStruct((B, SWA_HEADS, HEAD_DIM, S), BF16),
                   jax.ShapeDtypeStruct((B, S, LANES), BF16),
                   jax.ShapeDtypeStruct((B, G, S // BLOCK, SWA_V_ROWS, BLOCK), BF16),
                   jax.ShapeDtypeStruct((B, H, LANES, S), BF16),
                   jax.ShapeDtypeStruct((B, H, S, LANES), BF16),
                   jax.ShapeDtypeStruct((B, H, S // MLA_K_ROWS, MLA_V_ROWS, MLA_K_ROWS), BF16)],
        compiler_params=pltpu.CompilerParams(
            dimension_semantics=("arbitrary", "arbitrary"), vmem_limit_bytes=VMEM_LIMIT_BYTES),
        name="in_proj",
    )(x, g, win, wqt, qg, wuqt, kvg, wuk, wuvt, swa_tab, swat_tab, mk_tab, mqt_tab)


def _swa_kernel(sink_ref, qt_ref, k_ref, vt_ref, o_ref):
    n = pl.program_id(1)
    n_blk = SWA_Q_ROWS // BLOCK
    width = SWA_GROUP * BLOCK
    key = lax.broadcasted_iota(jnp.int32, (2 * BLOCK, width), 0)
    qry = lax.broadcasted_iota(jnp.int32, (2 * BLOCK, width), 1) & (BLOCK - 1)
    band = (key > qry) & (key <= qry + BLOCK)
    bias = jnp.where(band, 0.0, -jnp.inf).astype(F32)
    zeros = jnp.zeros((HEAD_DIM, width), BF16)

    def scores(unit):
        j, g = unit
        blk = n * n_blk + j
        prev0 = pl.multiple_of(jnp.maximum(blk - 1, 0) * BLOCK, BLOCK)
        cur0 = pl.multiple_of(blk * BLOCK, BLOCK)
        kw = jnp.concatenate([k_ref[pl.ds(prev0, BLOCK), :], k_ref[pl.ds(cur0, BLOCK), :]], axis=0)
        q = jnp.concatenate([qt_ref[SWA_GROUP * g + i, :, j * BLOCK:(j + 1) * BLOCK]
                             for i in range(SWA_GROUP)], axis=1)
        q = jnp.concatenate([q, zeros] if g == 0 else [zeros, q], axis=0)
        s = jnp.dot(kw, q, preferred_element_type=F32) + bias
        if j == 0:
            s = jnp.where((key >= BLOCK) | (blk > 0), s, -jnp.inf)
        return s

    def weighted_values(unit, p, m):
        j, g = unit
        blk = n * n_blk + j
        vw = jnp.concatenate([vt_ref[g, jnp.maximum(blk - 1, 0)], vt_ref[g, blk]], axis=1)
        o = jnp.dot(vw, p, preferred_element_type=F32)
        denom = o[HEAD_DIM:HEAD_DIM + 1] + jnp.exp2(sink_ref[g] - m)
        return o[:HEAD_DIM] / denom

    def store(j, outs):
        ot = jnp.concatenate([outs[g][:, i * BLOCK:(i + 1) * BLOCK]
                              for g in range(SWA_KV_HEADS) for i in range(SWA_GROUP)], axis=0)
        o_ref[j * BLOCK:(j + 1) * BLOCK, :] = ot.T.astype(BF16)

    def softmax(unit, s):
        m = jnp.maximum(jnp.max(s, axis=0, keepdims=True), sink_ref[unit[1]])
        return jnp.exp2(s - m).astype(BF16), m

    outs = {}

    def finish(unit, p, m):
        j, g = unit
        outs[g] = weighted_values(unit, p, m)
        if g == SWA_KV_HEADS - 1:
            store(j, outs)

    _software_pipeline([(j, g) for j in range(n_blk) for g in range(SWA_KV_HEADS)],
                       SWA_LOOKAHEAD, scores, softmax, finish)


def _swa_attention(sink_rows, qat, ka, vat):
    B, H, _, S = qat.shape
    G = SWA_KV_HEADS
    rows = SWA_Q_ROWS
    return pl.pallas_call(
        _swa_kernel,
        grid=(B, S // rows),
        in_specs=[pl.BlockSpec(sink_rows.shape, lambda b, n: (0, 0, 0)),
                  pl.BlockSpec((None, H, HEAD_DIM, rows), lambda b, n: (b, 0, 0, n)),
                  pl.BlockSpec((None, S, LANES), lambda b, n: (b, 0, 0)),
                  pl.BlockSpec((None, G, S // BLOCK, SWA_V_ROWS, BLOCK), lambda b, n: (b, 0, 0, 0, 0))],
        out_specs=pl.BlockSpec((None, rows, H * HEAD_DIM), lambda b, n: (b, n, 0)),
        out_shape=jax.ShapeDtypeStruct((B, S, H * HEAD_DIM), BF16),
        compiler_params=pltpu.CompilerParams(
            dimension_semantics=("arbitrary", "arbitrary"), vmem_limit_bytes=VMEM_LIMIT_BYTES),
        name="swa_attention",
    )(sink_rows, qat, ka, vat)


def _mla_kernel(qt_ref, k_ref, vt_ref, o_ref, m_ref, acc_ref):
    n = pl.program_id(1)
    tq, tk = MLA_Q_COLS, MLA_K_ROWS
    per_tile = tq // tk

    m_ref[...] = jnp.full(m_ref.shape, -jnp.inf, F32)
    acc_ref[...] = jnp.zeros(acc_ref.shape, F32)

    def scores(unit):
        j, h, c0, masked = unit
        k0 = pl.multiple_of(j * tk, tk)
        s = jnp.dot(k_ref[h, pl.ds(k0, tk), :], qt_ref[h, :, c0:], preferred_element_type=F32)
        if masked:
            key = lax.broadcasted_iota(jnp.int32, s.shape, 0)
            qry = lax.broadcasted_iota(jnp.int32, s.shape, 1)
            s = jnp.where(key + (j * tk - n * tq - c0) <= qry, s, -jnp.inf)
        return s

    def weighted_values(unit, p, alpha):
        j, h, c0, _ = unit
        acc_ref[h, :, c0:] = alpha * acc_ref[h, :, c0:] + jnp.dot(vt_ref[h, j], p, preferred_element_type=F32)

    def softmax(unit, s):
        _, h, c0, _ = unit
        m_prev = m_ref[h, :, c0:]
        m_new = jnp.maximum(m_prev, jnp.max(s, axis=0, keepdims=True))
        m_ref[h, :, c0:] = m_new
        return jnp.exp2(s - m_new).astype(BF16), jnp.exp2(m_prev - m_new)

    def run(units):
        _software_pipeline(units, MLA_LOOKAHEAD, scores, softmax, weighted_values)

    def body(i, carry):
        run([(i * per_tile + d, h, 0, False) for d in range(per_tile) for h in range(MLA_HEADS)])
        return carry

    lax.fori_loop(0, n, body, 0)
    run([(n * per_tile + d, h, d * tk, True) for d in range(per_tile) for h in range(MLA_HEADS)])

    ot = jnp.concatenate([acc_ref[h, :MLA_V_DIM] / acc_ref[h, MLA_V_DIM:MLA_V_DIM + 1]
                          for h in range(MLA_HEADS)], axis=0)
    o_ref[...] = ot.T.astype(BF16)


def _mla_attention(qt, km, vt):
    B, H, _, S = qt.shape
    tq, tk = MLA_Q_COLS, MLA_K_ROWS
    assert tq % tk == 0
    return pl.pallas_call(
        _mla_kernel,
        grid=(B, S // tq),
        in_specs=[pl.BlockSpec((None, H, LANES, tq), lambda b, n: (b, 0, 0, n)),
                  pl.BlockSpec((None, H, S, LANES), lambda b, n: (b, 0, 0, 0)),
                  pl.BlockSpec((None, H, S // tk, MLA_V_ROWS, tk), lambda b, n: (b, 0, 0, 0, 0))],
        out_specs=pl.BlockSpec((None, tq, H * MLA_V_DIM), lambda b, n: (b, n, 0)),
        out_shape=jax.ShapeDtypeStruct((B, S, H * MLA_V_DIM), BF16),
        scratch_shapes=[pltpu.VMEM((H, 1, tq), F32),
                        pltpu.VMEM((H, MLA_V_ROWS, tq), F32)],
        compiler_params=pltpu.CompilerParams(
            dimension_semantics=("arbitrary", "arbitrary"), vmem_limit_bytes=VMEM_LIMIT_BYTES),
        name="mla_attention",
    )(qt, km, vt)


def _post_kernel(x_ref, oa_ref, ob_ref, g1_ref, wg_ref, woa_ref, wob_ref, wout_ref,
                 g2_ref, wgate_ref, wup_ref, wdown_ref, g3_ref, out_ref, *, final_norm):
    D = x_ref.shape[-1]
    dot = functools.partial(jnp.dot, preferred_element_type=F32)
    n_sub = x_ref.shape[0] // POST_SUB_ROWS
    for t0 in range(0, n_sub, POST_INTERLEAVE):
        tiles = [slice((t0 + i) * POST_SUB_ROWS, (t0 + i + 1) * POST_SUB_ROWS) for i in range(POST_INTERLEAVE)]
        xs = [x_ref[rs, :] for rs in tiles]
        hs = [_rmsnorm_f32(x, g1_ref[...]).astype(BF16) for x in xs]
        gates = [dot(h, wg_ref[...]) for h in hs]
        yas = [dot(oa_ref[rs, :], woa_ref[...]) for rs in tiles]
        ybs = [dot(ob_ref[rs, :], wob_ref[...]) for rs in tiles]
        ys = [(jax.nn.sigmoid(g[:, :D]) * ya + jax.nn.sigmoid(g[:, D:]) * yb).astype(BF16)
              for g, ya, yb in zip(gates, yas, ybs)]
        x1s = [x + dot(y, wout_ref[...]) for x, y in zip(xs, ys)]

        h2s = [_rmsnorm_f32(x1, g2_ref[...]).astype(BF16) for x1 in x1s]
        hgs = [dot(h2, wgate_ref[...]) for h2 in h2s]
        hus = [dot(h2, wup_ref[...]) for h2 in h2s]
        acts = [(hg * jax.nn.sigmoid(hg) * hu).astype(BF16) for hg, hu in zip(hgs, hus)]
        x2s = [x1 + dot(a, wdown_ref[...]) for x1, a in zip(x1s, acts)]
        for rs, x2 in zip(tiles, x2s):
            out_ref[rs, :] = _rmsnorm_f32(x2, g3_ref[...]) if final_norm else x2


def _post(x, oa, ob, g1, wg, woa, wob, wout, g2, wgate, wup, wdown, g3, final_norm):
    B, S, D = x.shape
    rows = POST_ROWS
    tok = lambda w: pl.BlockSpec((None, rows, w), lambda b, i: (b, i, 0))
    const = lambda a: pl.BlockSpec(a.shape, lambda b, i: (0,) * a.ndim, pipeline_mode=pl.Buffered(1))
    return pl.pallas_call(
        functools.partial(_post_kernel, final_norm=final_norm),
        grid=(B, S // rows),
        in_specs=[tok(D), tok(oa.shape[-1]), tok(ob.shape[-1])]
                 + [const(a) for a in (g1, wg, woa, wob, wout, g2, wgate, wup, wdown, g3)],
        out_specs=tok(D),
        out_shape=jax.ShapeDtypeStruct((B, S, D), F32),
        compiler_params=pltpu.CompilerParams(
            dimension_semantics=("arbitrary", "arbitrary"), vmem_limit_bytes=VMEM_LIMIT_BYTES),
        name="post_attention",
    )(x, oa, ob, g1, wg, woa, wob, wout, g2, wgate, wup, wdown, g3)


def _rope_angles(seq, dim):
    inv = ROPE_THETA ** (-jnp.arange(0, dim, 2, dtype=F32) / dim)
    ang = jnp.arange(seq, dtype=F32)[:, None] * inv[None, :]
    return jnp.cos(ang), jnp.sin(ang)


def _rope_table(seq, dim, group_offsets):
    half = dim // 2
    cos, sin = _rope_angles(seq, dim)
    one = jnp.ones((seq, 1), F32)
    zero = jnp.zeros((seq, 1), F32)
    c, s1, s2 = [], [], []
    lane = 0
    for o in group_offsets:
        pad = o - lane
        c += [jnp.tile(one, (1, pad)), cos, cos]
        s1 += [jnp.tile(zero, (1, pad)), -sin, jnp.tile(zero, (1, half))]
        s2 += [jnp.tile(zero, (1, pad)), jnp.tile(zero, (1, half)), sin]
        lane = o + dim
    pad = LANES - lane
    c.append(jnp.tile(one, (1, pad)))
    s1.append(jnp.tile(zero, (1, pad)))
    s2.append(jnp.tile(zero, (1, pad)))
    return jnp.stack([jnp.concatenate(t, axis=1) for t in (c, s1, s2)])


def kernel(x, mix_norm_g, w_in, swa_sinks, q_norm_g, w_uq, kv_norm_g, w_ukv, w_o_swa, w_o_mla,
           w_out, ffn_norm_g, w_gate, w_up, w_down, final_norm_g):
    B, S, D = x.shape
    depth = w_in.shape[0]
    assert S % SWA_Q_ROWS == 0 and S % PROJ_ROWS == 0 and S % POST_ROWS == 0 and S % MLA_Q_COLS == 0
    assert PROJ_ROWS % MLA_K_ROWS == 0

    swa_tab = _rope_table(S, HEAD_DIM, (0, HEAD_DIM))
    mk_tab = _rope_table(S, MLA_ROPE_DIM, (MLA_NOPE_DIM,))
    cos_a, sin_a = _rope_angles(S, HEAD_DIM)
    swat_tab = jnp.stack([cos_a.T, sin_a.T]) * SWA_Q_SCALE
    cos_m, sin_m = _rope_angles(S, MLA_ROPE_DIM)
    mqt_tab = jnp.stack([cos_m.T, sin_m.T]) * MLA_Q_SCALE
    row = lambda v: v.reshape(1, -1)

    for l in range(depth):
        wi = w_in[l]
        wkr = jnp.pad(wi[:, 1408:1440], ((0, 0), (MLA_NOPE_DIM, LANES - MLA_QK_DIM)))
        win = jnp.concatenate([wi[:, 768:1408], wi[:, 512:768], wkr], axis=1).astype(BF16)
        wqt = wi[:, :512].T.astype(BF16)
        wg = wi[:, 1440:].astype(BF16)
        sink_rows = jnp.repeat(swa_sinks[l].reshape(SWA_KV_HEADS, 1, SWA_GROUP) * LOG2_E, BLOCK, axis=2)
        wuqt = jnp.pad(w_uq[l].reshape(Q_LORA_RANK, MLA_HEADS, MLA_QK_DIM),
                       ((0, 0), (0, 0), (0, LANES - MLA_QK_DIM))).reshape(Q_LORA_RANK, -1).T.astype(BF16)
        wkv = w_ukv[l].reshape(KV_LORA_RANK, MLA_HEADS, MLA_NOPE_DIM + MLA_V_DIM)
        wuk = jnp.pad(wkv[:, :, :MLA_NOPE_DIM],
                      ((0, 0), (0, 0), (0, LANES - MLA_NOPE_DIM))).reshape(KV_LORA_RANK, -1).astype(BF16)
        wuvt = wkv[:, :, MLA_NOPE_DIM:].reshape(KV_LORA_RANK, -1).T.astype(BF16)

        qat, ka, vat, qt, km, vt = _in_proj(x, row(mix_norm_g[l]), win, wqt, row(q_norm_g[l]), wuqt,
                                            row(kv_norm_g[l]), wuk, wuvt, swa_tab, swat_tab, mk_tab, mqt_tab)
        oa = _swa_attention(sink_rows, qat, ka, vat)
        ob = _mla_attention(qt, km, vt)
        x = _post(x, oa, ob, row(mix_norm_g[l]), wg, w_o_swa[l].astype(BF16), w_o_mla[l].astype(BF16),
                  w_out[l].astype(BF16),
                  row(ffn_norm_g[l]), w_gate[l].astype(BF16), w_up[l].astype(BF16),
                  w_down[l].astype(BF16), row(final_norm_g), final_norm=(l == depth - 1))
    return x
```

```python
import functools
import math

import jax
import jax.numpy as jnp
from jax import lax
from jax.experimental import pallas as pl
from jax.experimental.pallas import tpu as pltpu

EPS = 1e-6
ROPE_THETA = 10000.0
BLOCK = 128
HEAD_DIM = 64
SWA_HEADS = 8
SWA_KV_HEADS = 2
SWA_GROUP = SWA_HEADS // SWA_KV_HEADS
MLA_HEADS = 8
MLA_NOPE_DIM = 64
MLA_ROPE_DIM = 32
MLA_V_DIM = 64
MLA_QK_DIM = MLA_NOPE_DIM + MLA_ROPE_DIM
Q_LORA_RANK = 384
KV_LORA_RANK = 256

LANES = 128
V7X_VMEM_BYTES = 64 * 1024 * 1024
VMEM_LIMIT_BYTES = 56 * 1024 * 1024

PROJ_ROWS = 1024
SWA_Q_ROWS = 2048
MLA_Q_COLS = 512
MLA_K_ROWS = 256
POST_ROWS = 1024
POST_SUB_ROWS = 256
POST_INTERLEAVE = 2
MLA_LOOKAHEAD = 1
SWA_LOOKAHEAD = 2

MLA_HEAD_PAD = LANES
MLA_ROPE_HALF = MLA_ROPE_DIM // 2
MLA_V_ROWS = MLA_V_DIM + 16
SWA_V_ROWS = HEAD_DIM + 16
LOG2_E = math.log2(math.e)
MLA_Q_SCALE = (MLA_QK_DIM ** -0.5) * LOG2_E
SWA_Q_SCALE = (HEAD_DIM ** -0.5) * LOG2_E

F32 = jnp.float32
BF16 = jnp.bfloat16
NT_DIMS = (((1,), (1,)), ((), ()))


def _rmsnorm_f32(x, g):
    return x * lax.rsqrt(jnp.mean(x * x, axis=-1, keepdims=True) + EPS) * g


def _software_pipeline(units, lookahead, first, middle, last):
    ahead = {i: first(units[i]) for i in range(min(lookahead, len(units)))}
    pending = None
    for i, unit in enumerate(units):
        if i + lookahead < len(units):
            ahead[i + lookahead] = first(units[i + lookahead])
        mid = middle(unit, ahead.pop(i))
        if pending is not None:
            last(pending[0], *pending[1])
        pending = (unit, mid)
    last(pending[0], *pending[1])


def _rope_lane_group(x, cos, sin_signed, half):
    up = pltpu.roll(x, LANES - half, axis=1)
    down = pltpu.roll(x, half, axis=1)
    first = (lax.broadcasted_iota(jnp.int32, x.shape, 1) & (2 * half - 1)) < half
    return x * cos + jnp.where(first, up, down) * sin_signed


def _in_proj_kernel(x_ref, g_ref, win_ref, wqt_ref, qg_ref, wuqt_ref, kvg_ref, wuk_ref, wuvt_ref,
                    swa_tab, swat_tab, mk_tab, mqt_tab,
                    qat_ref, ka_ref, vat_ref, qt_ref, km_ref, vt_ref):
    n_lat = Q_LORA_RANK + KV_LORA_RANK
    n_first = n_lat + LANES
    o_v, o_kr = 0, LANES
    sub = MLA_K_ROWS
    ones = jnp.ones((MLA_V_ROWS - MLA_V_DIM, sub), F32)
    r0, r1, r2 = MLA_NOPE_DIM, MLA_NOPE_DIM + MLA_ROPE_HALF, MLA_QK_DIM
    half = HEAD_DIM // 2

    for t in range(x_ref.shape[0] // sub):
        rs = slice(t * sub, (t + 1) * sub)
        h = _rmsnorm_f32(x_ref[rs, :], g_ref[...]).astype(BF16)
        lat = jnp.dot(h, win_ref[:, :n_first], preferred_element_type=F32)
        qat = lax.dot_general(wqt_ref[...], h, NT_DIMS, preferred_element_type=F32)
        p = jnp.dot(h, win_ref[:, n_first:], preferred_element_type=F32)
        cq = _rmsnorm_f32(lat[:, :Q_LORA_RANK], qg_ref[...]).astype(BF16)
        ckv = _rmsnorm_f32(lat[:, Q_LORA_RANK:n_lat], kvg_ref[...]).astype(BF16)
        qt = lax.dot_general(wuqt_ref[...], cq, NT_DIMS, preferred_element_type=F32)
        kb = jnp.dot(ckv, wuk_ref[...], preferred_element_type=F32)
        vt = lax.dot_general(wuvt_ref[...], ckv, NT_DIMS, preferred_element_type=F32)

        cos_a, sin_a = swat_tab[0, :, rs], swat_tab[1, :, rs]
        for hd in range(SWA_HEADS):
            x1, x2 = qat[hd * HEAD_DIM:hd * HEAD_DIM + half], qat[hd * HEAD_DIM + half:(hd + 1) * HEAD_DIM]
            qat_ref[hd, :, rs] = jnp.concatenate(
                [x1 * cos_a - x2 * sin_a, x2 * cos_a + x1 * sin_a], axis=0).astype(BF16)
        ka_ref[rs, :] = _rope_lane_group(lat[:, n_lat:], swa_tab[0, rs, :], swa_tab[1, rs, :], half).astype(BF16)
        vat = p[:, o_v:o_v + LANES].T
        for g in range(SWA_KV_HEADS):
            for i in range(sub // BLOCK):
                vat_ref[g, t * (sub // BLOCK) + i] = jnp.concatenate(
                    [vat[g * HEAD_DIM:(g + 1) * HEAD_DIM, i * BLOCK:(i + 1) * BLOCK], ones[:, :BLOCK]],
                    axis=0).astype(BF16)

        cos_q, sin_q = mqt_tab[0, :, rs], mqt_tab[1, :, rs]
        for hd in range(MLA_HEADS):
            qh = qt[hd * LANES:(hd + 1) * LANES]
            x1, x2 = qh[r0:r1], qh[r1:r2]
            qt_ref[hd, :, rs] = jnp.concatenate(
                [qh[:r0] * MLA_Q_SCALE, x1 * cos_q - x2 * sin_q, x2 * cos_q + x1 * sin_q, qh[r2:]],
                axis=0).astype(BF16)

        kr = _rope_lane_group(p[:, o_kr:o_kr + LANES], mk_tab[0, rs, :], mk_tab[1, rs, :], MLA_ROPE_HALF)
        for hd in range(MLA_HEADS):
            km_ref[hd, rs, :] = (kb[:, hd * LANES:(hd + 1) * LANES] + kr).astype(BF16)
            vt_ref[hd, t] = jnp.concatenate(
                [vt[hd * MLA_V_DIM:(hd + 1) * MLA_V_DIM], ones], axis=0).astype(BF16)


def _in_proj(x, g, win, wqt, qg, wuqt, kvg, wuk, wuvt, swa_tab, swat_tab, mk_tab, mqt_tab):
    B, S, D = x.shape
    rows = PROJ_ROWS
    ns = S // rows
    kc = rows // MLA_K_ROWS
    H = MLA_HEADS
    G = SWA_KV_HEADS
    tok = lambda w: pl.BlockSpec((None, rows, w), lambda si, b: (b, si, 0))
    const = lambda a: pl.BlockSpec(a.shape, lambda si, b: (0,) * a.ndim)
    tab = pl.BlockSpec((2, rows, LANES), lambda si, b: (0, si, 0))
    tabt = lambda half: pl.BlockSpec((2, half, rows), lambda si, b: (0, 0, si))
    return pl.pallas_call(
        _in_proj_kernel,
        grid=(ns, B),
        in_specs=[tok(D), const(g), const(win), const(wqt), const(qg), const(wuqt), const(kvg), const(wuk),
                  const(wuvt), tab, tabt(HEAD_DIM // 2), tab, tabt(MLA_ROPE_HALF)],
        out_specs=[pl.BlockSpec((None, SWA_HEADS, HEAD_DIM, rows), lambda si, b: (b, 0, 0, si)),
                   tok(LANES),
                   pl.BlockSpec((None, G, rows // BLOCK, SWA_V_ROWS, BLOCK), lambda si, b: (b, 0, si, 0, 0)),
                   pl.BlockSpec((None, H, LANES, rows), lambda si, b: (b, 0, 0, si)),
                   pl.BlockSpec((None, H, rows, LANES), lambda si, b: (b, 0, si, 0)),
                   pl.BlockSpec((None, H, kc, MLA_V_ROWS, MLA_K_ROWS), lambda si, b: (b, 0, si, 0, 0))],
        out_shape=[jax.ShapeDtypeStruct((B, SWA_HEADS, HEAD_DIM, S), BF16),
                   jax.ShapeDtypeStruct((B, S, LANES), BF16),
                   jax.ShapeDtypeStruct((B, G, S // BLOCK, SWA_V_ROWS, BLOCK), BF16),
                   jax.ShapeDtypeStruct((B, H, LANES, S), BF16),
                   jax.ShapeDtypeStruct((B, H, S, LANES), BF16),
                   jax.ShapeDtypeStruct((B, H, S // MLA_K_ROWS, MLA_V_ROWS, MLA_K_ROWS), BF16)],
        compiler_params=pltpu.CompilerParams(
            dimension_semantics=("arbitrary", "arbitrary"), vmem_limit_bytes=VMEM_LIMIT_BYTES),
        name="in_proj",
    )(x, g, win, wqt, qg, wuqt, kvg, wuk, wuvt, swa_tab, swat_tab, mk_tab, mqt_tab)


def _swa_kernel(sink_ref, qt_ref, k_ref, vt_ref, o_ref):
    n = pl.program_id(1)
    n_blk = SWA_Q_ROWS // BLOCK
    width = SWA_GROUP * BLOCK
    key = lax.broadcasted_iota(jnp.int32, (2 * BLOCK, width), 0)
    qry = lax.broadcasted_iota(jnp.int32, (2 * BLOCK, width), 1) & (BLOCK - 1)
    band = (key > qry) & (key <= qry + BLOCK)
    bias = jnp.where(band, 0.0, -jnp.inf).astype(F32)
    zeros = jnp.zeros((HEAD_DIM, width), BF16)

    def scores(unit):
        j, g = unit
        blk = n * n_blk + j
        prev0 = pl.multiple_of(jnp.maximum(blk - 1, 0) * BLOCK, BLOCK)
        cur0 = pl.multiple_of(blk * BLOCK, BLOCK)
        kw = jnp.concatenate([k_ref[pl.ds(prev0, BLOCK), :], k_ref[pl.ds(cur0, BLOCK), :]], axis=0)
        q = jnp.concatenate([qt_ref[SWA_GROUP * g + i, :, j * BLOCK:(j + 1) * BLOCK]
                             for i in range(SWA_GROUP)], axis=1)
        q = jnp.concatenate([q, zeros] if g == 0 else [zeros, q], axis=0)
        s = jnp.dot(kw, q, preferred_element_type=F32) + bias
        if j == 0:
            s = jnp.where((key >= BLOCK) | (blk > 0), s, -jnp.inf)
        return s

    def weighted_values(unit, p, m):
        j, g = unit
        blk = n * n_blk + j
        vw = jnp.concatenate([vt_ref[g, jnp.maximum(blk - 1, 0)], vt_ref[g, blk]], axis=1)
        o = jnp.dot(vw, p, preferred_element_type=F32)
        denom = o[HEAD_DIM:HEAD_DIM + 1] + jnp.exp2(sink_ref[g] - m)
        return o[:HEAD_DIM] / denom

    def store(j, outs):
        ot = jnp.concatenate([outs[g][:, i * BLOCK:(i + 1) * BLOCK]
                              for g in range(SWA_KV_HEADS) for i in range(SWA_GROUP)], axis=0)
        o_ref[j * BLOCK:(j + 1) * BLOCK, :] = ot.T.astype(BF16)

    def softmax(unit, s):
        m = jnp.maximum(jnp.max(s, axis=0, keepdims=True), sink_ref[unit[1]])
        return jnp.exp2(s - m).astype(BF16), m

    outs = {}

    def finish(unit, p, m):
        j, g = unit
        outs[g] = weighted_values(unit, p, m)
        if g == SWA_KV_HEADS - 1:
            store(j, outs)

    _software_pipeline([(j, g) for j in range(n_blk) for g in range(SWA_KV_HEADS)],
                       SWA_LOOKAHEAD, scores, softmax, finish)


def _swa_attention(sink_rows, qat, ka, vat):
    B, H, _, S = qat.shape
    G = SWA_KV_HEADS
    rows = SWA_Q_ROWS
    return pl.pallas_call(
        _swa_kernel,
        grid=(B, S // rows),
        in_specs=[pl.BlockSpec(sink_rows.shape, lambda b, n: (0, 0, 0)),
                  pl.BlockSpec((None, H, HEAD_DIM, rows), lambda b, n: (b, 0, 0, n)),
                  pl.BlockSpec((None, S, LANES), lambda b, n: (b, 0, 0)),
                  pl.BlockSpec((None, G, S // BLOCK, SWA_V_ROWS, BLOCK), lambda b, n: (b, 0, 0, 0, 0))],
        out_specs=pl.BlockSpec((None, rows, H * HEAD_DIM), lambda b, n: (b, n, 0)),
        out_shape=jax.ShapeDtypeStruct((B, S, H * HEAD_DIM), BF16),
        compiler_params=pltpu.CompilerParams(
            dimension_semantics=("arbitrary", "arbitrary"), vmem_limit_bytes=VMEM_LIMIT_BYTES),
        name="swa_attention",
    )(sink_rows, qat, ka, vat)


def _mla_kernel(qt_ref, k_ref, vt_ref, o_ref, m_ref, acc_ref):
    n = pl.program_id(1)
    tq, tk = MLA_Q_COLS, MLA_K_ROWS
    per_tile = tq // tk

    m_ref[...] = jnp.full(m_ref.shape, -jnp.inf, F32)
    acc_ref[...] = jnp.zeros(acc_ref.shape, F32)
    causal_bias = jnp.where(lax.broadcasted_iota(jnp.int32, (tk, tq), 0)
                            <= lax.broadcasted_iota(jnp.int32, (tk, tq), 1), 0.0, -jnp.inf).astype(F32)

    def scores(unit):
        j, h, c0, masked = unit
        k0 = pl.multiple_of(j * tk, tk)
        s = jnp.dot(k_ref[h, pl.ds(k0, tk), :], qt_ref[h, :, c0:], preferred_element_type=F32)
        if masked:
            s = s + causal_bias[:, :tq - c0]
        return s

    def weighted_values(unit, p, alpha):
        j, h, c0, masked = unit
        acc_ref[h, :, c0:] = alpha * acc_ref[h, :, c0:] + jnp.dot(vt_ref[h, j], p, preferred_element_type=F32)
        if masked and c0 == (per_tile - 1) * tk and h % 2 == 1:
            pair = jnp.concatenate([acc_ref[i, :MLA_V_DIM] / acc_ref[i, MLA_V_DIM:MLA_V_DIM + 1]
                                    for i in (h - 1, h)], axis=0)
            o_ref[:, (h - 1) * MLA_V_DIM:(h + 1) * MLA_V_DIM] = pair.astype(BF16).T

    def softmax(unit, s):
        _, h, c0, _ = unit
        m_prev = m_ref[h, :, c0:]
        m_new = jnp.maximum(m_prev, jnp.max(s, axis=0, keepdims=True))
        m_ref[h, :, c0:] = m_new
        return jnp.exp2(s - m_new).astype(BF16), jnp.exp2(m_prev - m_new)

    def run(units):
        _software_pipeline(units, MLA_LOOKAHEAD, scores, softmax, weighted_values)

    def below(i):
        return [(i * per_tile + d, h, 0, False) for d in range(per_tile) for h in range(MLA_HEADS)]

    diagonal = [(n * per_tile + d, h, d * tk, True) for d in range(per_tile) for h in range(MLA_HEADS)]

    trips = jnp.maximum(n - 1, 0) // 2

    def body(i, carry):
        run(below(2 * i) + below(2 * i + 1))
        return carry

    lax.fori_loop(0, trips, body, 0)
    for left in range(3):
        @pl.when(n - 2 * trips == left)
        def _():
            run([u for r in range(left) for u in below(2 * trips + r)] + diagonal)


def _mla_attention(qt, km, vt):
    B, H, _, S = qt.shape
    tq, tk = MLA_Q_COLS, MLA_K_ROWS
    assert tq % tk == 0
    return pl.pallas_call(
        _mla_kernel,
        grid=(B, S // tq),
        in_specs=[pl.BlockSpec((None, H, LANES, tq), lambda b, n: (b, 0, 0, n)),
                  pl.BlockSpec((None, H, S, LANES), lambda b, n: (b, 0, 0, 0)),
                  pl.BlockSpec((None, H, S // tk, MLA_V_ROWS, tk), lambda b, n: (b, 0, 0, 0, 0))],
        out_specs=pl.BlockSpec((None, tq, H * MLA_V_DIM), lambda b, n: (b, n, 0)),
        out_shape=jax.ShapeDtypeStruct((B, S, H * MLA_V_DIM), BF16),
        scratch_shapes=[pltpu.VMEM((H, 1, tq), F32),
                        pltpu.VMEM((H, MLA_V_ROWS, tq), F32)],
        compiler_params=pltpu.CompilerParams(
            dimension_semantics=("arbitrary", "arbitrary"), vmem_limit_bytes=VMEM_LIMIT_BYTES),
        name="mla_attention",
    )(qt, km, vt)


def _post_kernel(x_ref, oa_ref, ob_ref, g1_ref, wg_ref, woa_ref, wob_ref, wout_ref,
                 g2_ref, wgate_ref, wup_ref, wdown_ref, g3_ref, out_ref, *, final_norm):
    D = x_ref.shape[-1]
    dot = functools.partial(jnp.dot, preferred_element_type=F32)
    n_sub = x_ref.shape[0] // POST_SUB_ROWS
    for t0 in range(0, n_sub, POST_INTERLEAVE):
        tiles = [slice((t0 + i) * POST_SUB_ROWS, (t0 + i + 1) * POST_SUB_ROWS) for i in range(POST_INTERLEAVE)]
        xs = [x_ref[rs, :] for rs in tiles]
        hs = [_rmsnorm_f32(x, g1_ref[...]).astype(BF16) for x in xs]
        gates = [dot(h, wg_ref[...]) for h in hs]
        yas = [dot(oa_ref[rs, :], woa_ref[...]) for rs in tiles]
        ybs = [dot(ob_ref[rs, :], wob_ref[...]) for rs in tiles]
        ys = [(jax.nn.sigmoid(g[:, :D]) * ya + jax.nn.sigmoid(g[:, D:]) * yb).astype(BF16)
              for g, ya, yb in zip(gates, yas, ybs)]
        x1s = [x + dot(y, wout_ref[...]) for x, y in zip(xs, ys)]

        h2s = [_rmsnorm_f32(x1, g2_ref[...]).astype(BF16) for x1 in x1s]
        hgs = [dot(h2, wgate_ref[...]) for h2 in h2s]
        hus = [dot(h2, wup_ref[...]) for h2 in h2s]
        acts = [(hg * jax.nn.sigmoid(hg) * hu).astype(BF16) for hg, hu in zip(hgs, hus)]
        x2s = [x1 + dot(a, wdown_ref[...]) for x1, a in zip(x1s, acts)]
        for rs, x2 in zip(tiles, x2s):
            out_ref[rs, :] = _rmsnorm_f32(x2, g3_ref[...]) if final_norm else x2


def _post(x, oa, ob, g1, wg, woa, wob, wout, g2, wgate, wup, wdown, g3, final_norm):
    B, S, D = x.shape
    rows = POST_ROWS
    tok = lambda w: pl.BlockSpec((None, rows, w), lambda b, i: (b, i, 0))
    const = lambda a: pl.BlockSpec(a.shape, lambda b, i: (0,) * a.ndim, pipeline_mode=pl.Buffered(1))
    return pl.pallas_call(
        functools.partial(_post_kernel, final_norm=final_norm),
        grid=(B, S // rows),
        in_specs=[tok(D), tok(oa.shape[-1]), tok(ob.shape[-1])]
                 + [const(a) for a in (g1, wg, woa, wob, wout, g2, wgate, wup, wdown, g3)],
        out_specs=tok(D),
        out_shape=jax.ShapeDtypeStruct((B, S, D), F32),
        compiler_params=pltpu.CompilerParams(
            dimension_semantics=("arbitrary", "arbitrary"), vmem_limit_bytes=VMEM_LIMIT_BYTES),
        name="post_attention",
    )(x, oa, ob, g1, wg, woa, wob, wout, g2, wgate, wup, wdown, g3)


def _rope_angles_t(seq, dim):
    inv = ROPE_THETA ** (-jnp.arange(0, dim, 2, dtype=F32) / dim)
    ang = inv[:, None] * jnp.arange(seq, dtype=F32)[None, :]
    return jnp.cos(ang), jnp.sin(ang)


def _rope_table(cos_t, sin_t, lead, trail):
    cos, sin = cos_t.T, sin_t.T
    seq, half = cos.shape
    reps = (LANES - lead - trail) // (2 * half)
    c = [jnp.ones((seq, lead), F32)] + [cos] * (2 * reps) + [jnp.ones((seq, trail), F32)]
    s = [jnp.zeros((seq, lead), F32)] + [-sin, sin] * reps + [jnp.zeros((seq, trail), F32)]
    return jnp.stack([jnp.concatenate(c, axis=1), jnp.concatenate(s, axis=1)])


def kernel(x, mix_norm_g, w_in, swa_sinks, q_norm_g, w_uq, kv_norm_g, w_ukv, w_o_swa, w_o_mla,
           w_out, ffn_norm_g, w_gate, w_up, w_down, final_norm_g):
    B, S, D = x.shape
    depth = w_in.shape[0]
    assert S % SWA_Q_ROWS == 0 and S % PROJ_ROWS == 0 and S % POST_ROWS == 0 and S % MLA_Q_COLS == 0
    assert PROJ_ROWS % MLA_K_ROWS == 0

    cos_a, sin_a, cos_m, sin_m = lax.optimization_barrier(
        _rope_angles_t(S, HEAD_DIM) + _rope_angles_t(S, MLA_ROPE_DIM))
    swa_tab = _rope_table(cos_a, sin_a, 0, 0)
    mk_tab = _rope_table(cos_m, sin_m, MLA_NOPE_DIM, LANES - MLA_QK_DIM)
    swat_tab = jnp.stack([cos_a, sin_a]) * SWA_Q_SCALE
    mqt_tab = jnp.stack([cos_m, sin_m]) * MLA_Q_SCALE
    row = lambda v: v.reshape(1, -1)

    for l in range(depth):
        wi = w_in[l]
        wkr = jnp.pad(wi[:, 1408:1440], ((0, 0), (MLA_NOPE_DIM, LANES - MLA_QK_DIM)))
        win = jnp.concatenate([wi[:, 768:1408], wi[:, 512:768], wkr], axis=1).astype(BF16)
        wqt = wi[:, :512].T.astype(BF16)
        wg = wi[:, 1440:].astype(BF16)
        sink_rows = jnp.repeat(swa_sinks[l].reshape(SWA_KV_HEADS, 1, SWA_GROUP) * LOG2_E, BLOCK, axis=2)
        wuqt = jnp.pad(w_uq[l].reshape(Q_LORA_RANK, MLA_HEADS, MLA_QK_DIM),
                       ((0, 0), (0, 0), (0, LANES - MLA_QK_DIM))).reshape(Q_LORA_RANK, -1).T.astype(BF16)
        wkv = w_ukv[l].reshape(KV_LORA_RANK, MLA_HEADS, MLA_NOPE_DIM + MLA_V_DIM)
        wuk = jnp.pad(wkv[:, :, :MLA_NOPE_DIM],
                      ((0, 0), (0, 0), (0, LANES - MLA_NOPE_DIM))).reshape(KV_LORA_RANK, -1).astype(BF16)
        wuvt = wkv[:, :, MLA_NOPE_DIM:].reshape(KV_LORA_RANK, -1).T.astype(BF16)

        qat, ka, vat, qt, km, vt = _in_proj(x, row(mix_norm_g[l]), win, wqt, row(q_norm_g[l]), wuqt,
                                            row(kv_norm_g[l]), wuk, wuvt, swa_tab, swat_tab, mk_tab, mqt_tab)
        oa = _swa_attention(sink_rows, qat, ka, vat)
        ob = _mla_attention(qt, km, vt)
        x = _post(x, oa, ob, row(mix_norm_g[l]), wg, w_o_swa[l].astype(BF16), w_o_mla[l].astype(BF16),
                  w_out[l].astype(BF16),
                  row(ffn_norm_g[l]), w_gate[l].astype(BF16), w_up[l].astype(BF16),
                  w_down[l].astype(BF16), row(final_norm_g), final_norm=(l == depth - 1))
    return x
```

```python
import functools
import math

import jax
import jax.numpy as jnp
from jax import lax
from jax.experimental import pallas as pl
from jax.experimental.pallas import tpu as pltpu

EPS = 1e-6
ROPE_THETA = 10000.0
BLOCK = 128
HEAD_DIM = 64
SWA_HEADS = 8
SWA_KV_HEADS = 2
SWA_GROUP = SWA_HEADS // SWA_KV_HEADS
MLA_HEADS = 8
MLA_NOPE_DIM = 64
MLA_ROPE_DIM = 32
MLA_V_DIM = 64
MLA_QK_DIM = MLA_NOPE_DIM + MLA_ROPE_DIM
Q_LORA_RANK = 384
KV_LORA_RANK = 256

LANES = 128
BF16_SUBLANES = 16
V7X_VMEM_BYTES = 64 * 1024 * 1024
VMEM_LIMIT_BYTES = 56 * 1024 * 1024

PROJ_ROWS = 1024
SWA_Q_ROWS = 2048
MLA_Q_COLS = 512
MLA_K_ROWS = 256
POST_ROWS = 1024
POST_SUB_ROWS = 256
POST_INTERLEAVE = 2
MLA_LOOKAHEAD = 2
SWA_LOOKAHEAD = 2

MLA_HEAD_PAD = LANES
MLA_ROPE_HALF = MLA_ROPE_DIM // 2
MLA_V_ROWS = MLA_V_DIM + 16
SWA_V_ROWS = HEAD_DIM + 16
LOG2_E = math.log2(math.e)
MLA_Q_SCALE = (MLA_QK_DIM ** -0.5) * LOG2_E
SWA_Q_SCALE = (HEAD_DIM ** -0.5) * LOG2_E

F32 = jnp.float32
BF16 = jnp.bfloat16
NT_DIMS = (((1,), (1,)), ((), ()))


def _rmsnorm_f32(x, g):
    return x * lax.rsqrt(jnp.mean(x * x, axis=-1, keepdims=True) + EPS) * g


def _software_pipeline(units, lookahead, first, middle, last):
    ahead = {i: first(units[i]) for i in range(min(lookahead, len(units)))}
    pending = None
    for i, unit in enumerate(units):
        if i + lookahead < len(units):
            ahead[i + lookahead] = first(units[i + lookahead])
        mid = middle(unit, ahead.pop(i))
        if pending is not None:
            last(pending[0], *pending[1])
        pending = (unit, mid)
    last(pending[0], *pending[1])


def _rope_lane_group(x, cos, sin_signed, half):
    up = pltpu.roll(x, LANES - half, axis=1)
    down = pltpu.roll(x, half, axis=1)
    first = (lax.broadcasted_iota(jnp.int32, x.shape, 1) & (2 * half - 1)) < half
    return x * cos + jnp.where(first, up, down) * sin_signed


def _in_proj_kernel(x_ref, g_ref, win_ref, wqt_ref, qg_ref, wuqt_ref, kvg_ref, wuk_ref, wuvt_ref,
                    swa_tab, swat_tab, mk_tab, mqt_tab,
                    qat_ref, ka_ref, vat_ref, qt_ref, km_ref, vt_ref):
    n_lat = Q_LORA_RANK + KV_LORA_RANK
    n_first = n_lat + LANES
    o_v, o_kr = 0, LANES
    sub = MLA_K_ROWS
    ones = jnp.ones((MLA_V_ROWS - MLA_V_DIM, sub), F32)
    r0, r1, r2 = MLA_NOPE_DIM, MLA_NOPE_DIM + MLA_ROPE_HALF, MLA_QK_DIM
    half = HEAD_DIM // 2

    for t in range(x_ref.shape[0] // sub):
        rs = slice(t * sub, (t + 1) * sub)
        h = _rmsnorm_f32(x_ref[rs, :], g_ref[...]).astype(BF16)
        lat = jnp.dot(h, win_ref[:, :n_first], preferred_element_type=F32)
        qat = lax.dot_general(wqt_ref[...], h, NT_DIMS, preferred_element_type=F32)
        p = jnp.dot(h, win_ref[:, n_first:], preferred_element_type=F32)
        cq = _rmsnorm_f32(lat[:, :Q_LORA_RANK], qg_ref[...]).astype(BF16)
        ckv = _rmsnorm_f32(lat[:, Q_LORA_RANK:n_lat], kvg_ref[...]).astype(BF16)
        qt = lax.dot_general(wuqt_ref[...], cq, NT_DIMS, preferred_element_type=F32)
        kb = jnp.dot(ckv, wuk_ref[...], preferred_element_type=F32)
        vt = lax.dot_general(wuvt_ref[...], ckv, NT_DIMS, preferred_element_type=F32)

        cos_a, sin_a = swat_tab[0, :, rs], swat_tab[1, :, rs]
        for hd in range(SWA_HEADS):
            x1, x2 = qat[hd * HEAD_DIM:hd * HEAD_DIM + half], qat[hd * HEAD_DIM + half:(hd + 1) * HEAD_DIM]
            qat_ref[hd, :, rs] = jnp.concatenate(
                [x1 * cos_a - x2 * sin_a, x2 * cos_a + x1 * sin_a], axis=0).astype(BF16)
        ka_ref[rs, :] = _rope_lane_group(lat[:, n_lat:], swa_tab[0, rs, :], swa_tab[1, rs, :], half).astype(BF16)
        vat = p[:, o_v:o_v + LANES].T
        for g in range(SWA_KV_HEADS):
            for i in range(sub // BLOCK):
                vat_ref[g, t * (sub // BLOCK) + i] = jnp.concatenate(
                    [vat[g * HEAD_DIM:(g + 1) * HEAD_DIM, i * BLOCK:(i + 1) * BLOCK], ones[:, :BLOCK]],
                    axis=0).astype(BF16)

        cos_q, sin_q = mqt_tab[0, :, rs], mqt_tab[1, :, rs]
        for hd in range(MLA_HEADS):
            qh = qt[hd * LANES:(hd + 1) * LANES]
            x1, x2 = qh[r0:r1], qh[r1:r2]
            qt_ref[hd, :, rs] = jnp.concatenate(
                [qh[:r0] * MLA_Q_SCALE, x1 * cos_q - x2 * sin_q, x2 * cos_q + x1 * sin_q, qh[r2:]],
                axis=0).astype(BF16)

        kr = _rope_lane_group(p[:, o_kr:o_kr + LANES], mk_tab[0, rs, :], mk_tab[1, rs, :], MLA_ROPE_HALF)
        for hd in range(MLA_HEADS):
            km_ref[hd, rs, :] = (kb[:, hd * LANES:(hd + 1) * LANES] + kr).astype(BF16)
            vt_ref[hd, t] = jnp.concatenate(
                [vt[hd * MLA_V_DIM:(hd + 1) * MLA_V_DIM], ones], axis=0).astype(BF16)


def _in_proj(x, g, win, wqt, qg, wuqt, kvg, wuk, wuvt, swa_tab, swat_tab, mk_tab, mqt_tab):
    B, S, D = x.shape
    rows = PROJ_ROWS
    ns = S // rows
    kc = rows // MLA_K_ROWS
    H = MLA_HEADS
    G = SWA_KV_HEADS
    tok = lambda w: pl.BlockSpec((None, rows, w), lambda si, b: (b, si, 0))
    const = lambda a: pl.BlockSpec(a.shape, lambda si, b: (0,) * a.ndim)
    tab = pl.BlockSpec((2, rows, LANES), lambda si, b: (0, si, 0))
    tabt = lambda half: pl.BlockSpec((2, half, rows), lambda si, b: (0, 0, si))
    return pl.pallas_call(
        _in_proj_kernel,
        grid=(ns, B),
        in_specs=[tok(D), const(g), const(win), const(wqt), const(qg), const(wuqt), const(kvg), const(wuk),
                  const(wuvt), tab, tabt(HEAD_DIM // 2), tab, tabt(MLA_ROPE_HALF)],
        out_specs=[pl.BlockSpec((None, SWA_HEADS, HEAD_DIM, rows), lambda si, b: (b, 0, 0, si)),
                   tok(LANES),
                   pl.BlockSpec((None, G, rows // BLOCK, SWA_V_ROWS, BLOCK), lambda si, b: (b, 0, si, 0, 0)),
                   pl.BlockSpec((None, H, LANES, rows), lambda si, b: (b, 0, 0, si)),
                   pl.BlockSpec((None, H, rows, LANES), lambda si, b: (b, 0, si, 0)),
                   pl.BlockSpec((None, H, kc, MLA_V_ROWS, MLA_K_ROWS), lambda si, b: (b, 0, si, 0, 0))],
        out_shape=[jax.ShapeDtypeStruct((B, SWA_HEADS, HEAD_DIM, S), BF16),
                   jax.ShapeDtypeStruct((B, S, LANES), BF16),
                   jax.ShapeDtypeStruct((B, G, S // BLOCK, SWA_V_ROWS, BLOCK), BF16),
                   jax.ShapeDtypeStruct((B, H, LANES, S), BF16),
                   jax.ShapeDtypeStruct((B, H, S, LANES), BF16),
                   jax.ShapeDtypeStruct((B, H, S // MLA_K_ROWS, MLA_V_ROWS, MLA_K_ROWS), BF16)],
        compiler_params=pltpu.CompilerParams(
            dimension_semantics=("arbitrary", "arbitrary"), vmem_limit_bytes=VMEM_LIMIT_BYTES),
        name="in_proj",
    )(x, g, win, wqt, qg, wuqt, kvg, wuk, wuvt, swa_tab, swat_tab, mk_tab, mqt_tab)


def _swa_kernel(sink_ref, qt_ref, k_ref, vt_ref, *refs, n_cast):
    o_ref = refs[n_cast]
    for src, dst in zip(refs[:n_cast], refs[n_cast + 1:]):
        dst[...] = src[...].astype(BF16)
    n = pl.program_id(1)
    n_blk = SWA_Q_ROWS // BLOCK
    width = SWA_GROUP * BLOCK
    key = lax.broadcasted_iota(jnp.int32, (2 * BLOCK, width), 0)
    qry = lax.broadcasted_iota(jnp.int32, (2 * BLOCK, width), 1) & (BLOCK - 1)
    band = (key > qry) & (key <= qry + BLOCK)
    bias = jnp.where(band, 0.0, -jnp.inf).astype(F32)
    zeros = jnp.zeros((HEAD_DIM, width), BF16)

    def scores(unit):
        j, g = unit
        blk = n * n_blk + j
        prev0 = pl.multiple_of(jnp.maximum(blk - 1, 0) * BLOCK, BLOCK)
        cur0 = pl.multiple_of(blk * BLOCK, BLOCK)
        kw = jnp.concatenate([k_ref[pl.ds(prev0, BLOCK), :], k_ref[pl.ds(cur0, BLOCK), :]], axis=0)
        q = jnp.concatenate([qt_ref[SWA_GROUP * g + i, :, j * BLOCK:(j + 1) * BLOCK]
                             for i in range(SWA_GROUP)], axis=1)
        q = jnp.concatenate([q, zeros] if g == 0 else [zeros, q], axis=0)
        s = jnp.dot(kw, q, preferred_element_type=F32) + bias
        if j == 0:
            s = jnp.where((key >= BLOCK) | (blk > 0), s, -jnp.inf)
        return s

    def weighted_values(unit, p, m):
        j, g = unit
        blk = n * n_blk + j
        vw = jnp.concatenate([vt_ref[g, jnp.maximum(blk - 1, 0)], vt_ref[g, blk]], axis=1)
        o = jnp.dot(vw, p, preferred_element_type=F32)
        denom = o[HEAD_DIM:HEAD_DIM + 1] + jnp.exp2(sink_ref[g] - m)
        return o[:HEAD_DIM] / denom

    def store(j, outs):
        ot = jnp.concatenate([outs[g][:, i * BLOCK:(i + 1) * BLOCK]
                              for g in range(SWA_KV_HEADS) for i in range(SWA_GROUP)], axis=0)
        o_ref[j * BLOCK:(j + 1) * BLOCK, :] = ot.T.astype(BF16)

    def softmax(unit, s):
        m = jnp.maximum(jnp.max(s, axis=0, keepdims=True), sink_ref[unit[1]])
        return jnp.exp2(s - m).astype(BF16), m

    outs = {}

    def finish(unit, p, m):
        j, g = unit
        outs[g] = weighted_values(unit, p, m)
        if g == SWA_KV_HEADS - 1:
            store(j, outs)

    _software_pipeline([(j, g) for j in range(n_blk) for g in range(SWA_KV_HEADS)],
                       SWA_LOOKAHEAD, scores, softmax, finish)


def _swa_attention(sink_rows, qat, ka, vat, riders):
    B, H, _, S = qat.shape
    G = SWA_KV_HEADS
    rows = SWA_Q_ROWS
    per_batch = S // rows
    steps = B * per_batch

    def slab(w):
        assert w.shape[0] % (steps * BF16_SUBLANES) == 0, w.shape
        return pl.BlockSpec((w.shape[0] // steps, w.shape[1]), lambda b, n: (b * per_batch + n, 0))

    out = pl.pallas_call(
        functools.partial(_swa_kernel, n_cast=len(riders)),
        grid=(B, per_batch),
        in_specs=[pl.BlockSpec(sink_rows.shape, lambda b, n: (0, 0, 0)),
                  pl.BlockSpec((None, H, HEAD_DIM, rows), lambda b, n: (b, 0, 0, n)),
                  pl.BlockSpec((None, S, LANES), lambda b, n: (b, 0, 0)),
                  pl.BlockSpec((None, G, S // BLOCK, SWA_V_ROWS, BLOCK), lambda b, n: (b, 0, 0, 0, 0))]
                 + [slab(w) for w in riders],
        out_specs=[pl.BlockSpec((None, rows, H * HEAD_DIM), lambda b, n: (b, n, 0))] + [slab(w) for w in riders],
        out_shape=[jax.ShapeDtypeStruct((B, S, H * HEAD_DIM), BF16)]
                  + [jax.ShapeDtypeStruct(w.shape, BF16) for w in riders],
        compiler_params=pltpu.CompilerParams(
            dimension_semantics=("arbitrary", "arbitrary"), vmem_limit_bytes=VMEM_LIMIT_BYTES),
        name="swa_attention",
    )(sink_rows, qat, ka, vat, *riders)
    return out[0], out[1:]


def _mla_kernel(qt_ref, k_ref, vt_ref, o_ref, m_ref, acc_ref):
    n = pl.program_id(1)
    tq, tk = MLA_Q_COLS, MLA_K_ROWS
    per_tile = tq // tk

    m_ref[...] = jnp.full(m_ref.shape, -jnp.inf, F32)
    acc_ref[...] = jnp.zeros(acc_ref.shape, F32)
    causal_bias = jnp.where(lax.broadcasted_iota(jnp.int32, (tk, tq), 0)
                            <= lax.broadcasted_iota(jnp.int32, (tk, tq), 1), 0.0, -jnp.inf).astype(F32)

    def scores(unit):
        j, h, c0, masked = unit
        k0 = pl.multiple_of(j * tk, tk)
        s = jnp.dot(k_ref[h, pl.ds(k0, tk), :], qt_ref[h, :, c0:], preferred_element_type=F32)
        if masked:
            s = s + causal_bias[:, :tq - c0]
        return s

    def weighted_values(unit, p, alpha):
        j, h, c0, masked = unit
        acc_ref[h, :, c0:] = alpha * acc_ref[h, :, c0:] + jnp.dot(vt_ref[h, j], p, preferred_element_type=F32)
        if masked and c0 == (per_tile - 1) * tk and h % 2 == 1:
            pair = jnp.concatenate([acc_ref[i, :MLA_V_DIM] / acc_ref[i, MLA_V_DIM:MLA_V_DIM + 1]
                                    for i in (h - 1, h)], axis=0)
            o_ref[:, (h - 1) * MLA_V_DIM:(h + 1) * MLA_V_DIM] = pair.astype(BF16).T

    def softmax(unit, s):
        _, h, c0, _ = unit
        m_prev = m_ref[h, :, c0:]
        m_new = jnp.maximum(m_prev, jnp.max(s, axis=0, keepdims=True))
        m_ref[h, :, c0:] = m_new
        return jnp.exp2(s - m_new).astype(BF16), jnp.exp2(m_prev - m_new)

    def run(units):
        _software_pipeline(units, MLA_LOOKAHEAD, scores, softmax, weighted_values)

    def below(i):
        return [(i * per_tile + d, h, 0, False) for d in range(per_tile) for h in range(MLA_HEADS)]

    diagonal = [(n * per_tile + d, h, d * tk, True) for d in range(per_tile) for h in range(MLA_HEADS)]

    trips = jnp.maximum(n - 1, 0) // 2

    def body(i, carry):
        run(below(2 * i) + below(2 * i + 1))
        return carry

    lax.fori_loop(0, trips, body, 0)
    for left in range(3):
        @pl.when(n - 2 * trips == left)
        def _():
            run([u for r in range(left) for u in below(2 * trips + r)] + diagonal)


def _mla_attention(qt, km, vt):
    B, H, _, S = qt.shape
    tq, tk = MLA_Q_COLS, MLA_K_ROWS
    assert tq % tk == 0
    return pl.pallas_call(
        _mla_kernel,
        grid=(B, S // tq),
        in_specs=[pl.BlockSpec((None, H, LANES, tq), lambda b, n: (b, 0, 0, n)),
                  pl.BlockSpec((None, H, S, LANES), lambda b, n: (b, 0, 0, 0)),
                  pl.BlockSpec((None, H, S // tk, MLA_V_ROWS, tk), lambda b, n: (b, 0, 0, 0, 0))],
        out_specs=pl.BlockSpec((None, tq, H * MLA_V_DIM), lambda b, n: (b, n, 0)),
        out_shape=jax.ShapeDtypeStruct((B, S, H * MLA_V_DIM), BF16),
        scratch_shapes=[pltpu.VMEM((H, 1, tq), F32),
                        pltpu.VMEM((H, MLA_V_ROWS, tq), F32)],
        compiler_params=pltpu.CompilerParams(
            dimension_semantics=("arbitrary", "arbitrary"), vmem_limit_bytes=VMEM_LIMIT_BYTES),
        name="mla_attention",
    )(qt, km, vt)


def _post_kernel(x_ref, oa_ref, ob_ref, g1_ref, wg_ref, woa_ref, wob_ref, wout_ref,
                 g2_ref, wgate_ref, wup_ref, wdown_ref, g3_ref, out_ref, *, final_norm):
    D = x_ref.shape[-1]
    dot = functools.partial(jnp.dot, preferred_element_type=F32)
    n_sub = x_ref.shape[0] // POST_SUB_ROWS
    for t0 in range(0, n_sub, POST_INTERLEAVE):
        tiles = [slice((t0 + i) * POST_SUB_ROWS, (t0 + i + 1) * POST_SUB_ROWS) for i in range(POST_INTERLEAVE)]
        xs = [x_ref[rs, :] for rs in tiles]
        hs = [_rmsnorm_f32(x, g1_ref[...]).astype(BF16) for x in xs]
        gates = [dot(h, wg_ref[...]) for h in hs]
        yas = [dot(oa_ref[rs, :], woa_ref[...]) for rs in tiles]
        ybs = [dot(ob_ref[rs, :], wob_ref[...]) for rs in tiles]
        ys = [(jax.nn.sigmoid(g[:, :D]) * ya + jax.nn.sigmoid(g[:, D:]) * yb).astype(BF16)
              for g, ya, yb in zip(gates, yas, ybs)]
        x1s = [x + dot(y, wout_ref[...]) for x, y in zip(xs, ys)]

        h2s = [_rmsnorm_f32(x1, g2_ref[...]).astype(BF16) for x1 in x1s]
        hgs = [dot(h2, wgate_ref[...]) for h2 in h2s]
        hus = [dot(h2, wup_ref[...]) for h2 in h2s]
        acts = [(hg * jax.nn.sigmoid(hg) * hu).astype(BF16) for hg, hu in zip(hgs, hus)]
        x2s = [x1 + dot(a, wdown_ref[...]) for x1, a in zip(x1s, acts)]
        for rs, x2 in zip(tiles, x2s):
            out_ref[rs, :] = _rmsnorm_f32(x2, g3_ref[...]) if final_norm else x2


def _post(x, oa, ob, g1, wg, woa, wob, wout, g2, wgate, wup, wdown, g3, final_norm):
    B, S, D = x.shape
    rows = POST_ROWS
    tok = lambda w: pl.BlockSpec((None, rows, w), lambda b, i: (b, i, 0))
    const = lambda a: pl.BlockSpec(a.shape, lambda b, i: (0,) * a.ndim, pipeline_mode=pl.Buffered(1))
    return pl.pallas_call(
        functools.partial(_post_kernel, final_norm=final_norm),
        grid=(B, S // rows),
        in_specs=[tok(D), tok(oa.shape[-1]), tok(ob.shape[-1])]
                 + [const(a) for a in (g1, wg, woa, wob, wout, g2, wgate, wup, wdown, g3)],
        out_specs=tok(D),
        out_shape=jax.ShapeDtypeStruct((B, S, D), F32),
        compiler_params=pltpu.CompilerParams(
            dimension_semantics=("arbitrary", "arbitrary"), vmem_limit_bytes=VMEM_LIMIT_BYTES),
        name="post_attention",
    )(x, oa, ob, g1, wg, woa, wob, wout, g2, wgate, wup, wdown, g3)


def _rope_angles_t(seq, dim):
    inv = ROPE_THETA ** (-jnp.arange(0, dim, 2, dtype=F32) / dim)
    ang = inv[:, None] * jnp.arange(seq, dtype=F32)[None, :]
    return jnp.cos(ang), jnp.sin(ang)


def _rope_table(cos_t, sin_t, lead, trail):
    cos, sin = cos_t.T, sin_t.T
    seq, half = cos.shape
    reps = (LANES - lead - trail) // (2 * half)
    c = [jnp.ones((seq, lead), F32)] + [cos] * (2 * reps) + [jnp.ones((seq, trail), F32)]
    s = [jnp.zeros((seq, lead), F32)] + [-sin, sin] * reps + [jnp.zeros((seq, trail), F32)]
    return jnp.stack([jnp.concatenate(c, axis=1), jnp.concatenate(s, axis=1)])


def kernel(x, mix_norm_g, w_in, swa_sinks, q_norm_g, w_uq, kv_norm_g, w_ukv, w_o_swa, w_o_mla,
           w_out, ffn_norm_g, w_gate, w_up, w_down, final_norm_g):
    B, S, D = x.shape
    depth = w_in.shape[0]
    assert S % SWA_Q_ROWS == 0 and S % PROJ_ROWS == 0 and S % POST_ROWS == 0 and S % MLA_Q_COLS == 0
    assert PROJ_ROWS % MLA_K_ROWS == 0

    cos_a, sin_a, cos_m, sin_m = lax.optimization_barrier(
        _rope_angles_t(S, HEAD_DIM) + _rope_angles_t(S, MLA_ROPE_DIM))
    swa_tab = _rope_table(cos_a, sin_a, 0, 0)
    mk_tab = _rope_table(cos_m, sin_m, MLA_NOPE_DIM, LANES - MLA_QK_DIM)
    swat_tab = jnp.stack([cos_a, sin_a]) * SWA_Q_SCALE
    mqt_tab = jnp.stack([cos_m, sin_m]) * MLA_Q_SCALE
    row = lambda v: v.reshape(1, -1)

    for l in range(depth):
        wi = w_in[l]
        wkr = jnp.pad(wi[:, 1408:1440], ((0, 0), (MLA_NOPE_DIM, LANES - MLA_QK_DIM)))
        win = jnp.concatenate([wi[:, 768:1408], wi[:, 512:768], wkr], axis=1).astype(BF16)
        wqt = wi[:, :512].T.astype(BF16)
        wg = wi[:, 1440:].astype(BF16)
        sink_rows = jnp.repeat(swa_sinks[l].reshape(SWA_KV_HEADS, 1, SWA_GROUP) * LOG2_E, BLOCK, axis=2)
        wuqt = jnp.pad(w_uq[l].reshape(Q_LORA_RANK, MLA_HEADS, MLA_QK_DIM),
                       ((0, 0), (0, 0), (0, LANES - MLA_QK_DIM))).reshape(Q_LORA_RANK, -1).T.astype(BF16)
        wkv = w_ukv[l].reshape(KV_LORA_RANK, MLA_HEADS, MLA_NOPE_DIM + MLA_V_DIM)
        wuk = jnp.pad(wkv[:, :, :MLA_NOPE_DIM],
                      ((0, 0), (0, 0), (0, LANES - MLA_NOPE_DIM))).reshape(KV_LORA_RANK, -1).astype(BF16)
        wuvt = wkv[:, :, MLA_NOPE_DIM:].reshape(KV_LORA_RANK, -1).T.astype(BF16)

        qat, ka, vat, qt, km, vt = _in_proj(x, row(mix_norm_g[l]), win, wqt, row(q_norm_g[l]), wuqt,
                                            row(kv_norm_g[l]), wuk, wuvt, swa_tab, swat_tab, mk_tab, mqt_tab)
        oa, (woa, wob, wout, wgate, wup, wdown) = _swa_attention(
            sink_rows, qat, ka, vat, (w_o_swa[l], w_o_mla[l], w_out[l], w_gate[l], w_up[l], w_down[l]))
        ob = _mla_attention(qt, km, vt)
        x = _post(x, oa, ob, row(mix_norm_g[l]), wg, woa, wob, wout, row(ffn_norm_g[l]), wgate, wup, wdown,
                  row(final_norm_g), final_norm=(l == depth - 1))
    return x
```

```python
import functools
import math

import jax
import jax.numpy as jnp
from jax import lax
from jax.experimental import pallas as pl
from jax.experimental.pallas import tpu as pltpu

EPS = 1e-6
ROPE_THETA = 10000.0
BLOCK = 128
HEAD_DIM = 64
SWA_HEADS = 8
SWA_KV_HEADS = 2
SWA_GROUP = SWA_HEADS // SWA_KV_HEADS
MLA_HEADS = 8
MLA_NOPE_DIM = 64
MLA_ROPE_DIM = 32
MLA_V_DIM = 64
MLA_QK_DIM = MLA_NOPE_DIM + MLA_ROPE_DIM
Q_LORA_RANK = 384
KV_LORA_RANK = 256

LANES = 128
BF16_SUBLANES = 16
V7X_VMEM_BYTES = 64 * 1024 * 1024
VMEM_LIMIT_BYTES = 56 * 1024 * 1024

PROJ_ROWS = 1024
SWA_Q_ROWS = 2048
MLA_Q_COLS = 512
MLA_K_ROWS = 256
POST_ROWS = 1024
POST_SUB_ROWS = 256
POST_INTERLEAVE = 2
MLA_LOOKAHEAD = 2
SWA_LOOKAHEAD = 2

MLA_ROPE_HALF = MLA_ROPE_DIM // 2
MLA_V_ROWS = MLA_V_DIM + 16
SWA_V_ROWS = HEAD_DIM + 16
LOG2_E = math.log2(math.e)
MLA_Q_SCALE = (MLA_QK_DIM ** -0.5) * LOG2_E
SWA_Q_SCALE = (HEAD_DIM ** -0.5) * LOG2_E

F32 = jnp.float32
BF16 = jnp.bfloat16
NT_DIMS = (((1,), (1,)), ((), ()))


def _rmsnorm_f32(x, g):
    return x * lax.rsqrt(jnp.mean(x * x, axis=-1, keepdims=True) + EPS) * g


def _software_pipeline(units, lookahead, first, middle, last):
    ahead = {i: first(units[i]) for i in range(min(lookahead, len(units)))}
    pending = None
    for i, unit in enumerate(units):
        if i + lookahead < len(units):
            ahead[i + lookahead] = first(units[i + lookahead])
        mid = middle(unit, ahead.pop(i))
        if pending is not None:
            last(pending[0], *pending[1])
        pending = (unit, mid)
    last(pending[0], *pending[1])


def _rope_lane_group(x, cos, sin_signed, half):
    up = pltpu.roll(x, LANES - half, axis=1)
    down = pltpu.roll(x, half, axis=1)
    first = (lax.broadcasted_iota(jnp.int32, x.shape, 1) & (2 * half - 1)) < half
    return x * cos + jnp.where(first, up, down) * sin_signed


def _in_proj_kernel(x_ref, g_ref, win_ref, wqt_ref, qg_ref, wuqt_ref, kvg_ref, wuk_ref, wuvt_ref,
                    swa_tab, swat_tab, mk_tab, mqt_tab,
                    qat_ref, ka_ref, vat_ref, qt_ref, km_ref, vt_ref):
    n_lat = Q_LORA_RANK + KV_LORA_RANK
    n_first = n_lat + LANES
    o_v, o_kr = 0, LANES
    sub = MLA_K_ROWS
    ones = jnp.ones((MLA_V_ROWS - MLA_V_DIM, sub), F32)
    r0, r1, r2 = MLA_NOPE_DIM, MLA_NOPE_DIM + MLA_ROPE_HALF, MLA_QK_DIM
    half = HEAD_DIM // 2

    for t in range(x_ref.shape[0] // sub):
        rs = slice(t * sub, (t + 1) * sub)
        h = _rmsnorm_f32(x_ref[rs, :], g_ref[...]).astype(BF16)
        lat = jnp.dot(h, win_ref[:, :n_first], preferred_element_type=F32)
        qat = lax.dot_general(wqt_ref[...], h, NT_DIMS, preferred_element_type=F32)
        p = jnp.dot(h, win_ref[:, n_first:], preferred_element_type=F32)
        cq = _rmsnorm_f32(lat[:, :Q_LORA_RANK], qg_ref[...]).astype(BF16)
        ckv = _rmsnorm_f32(lat[:, Q_LORA_RANK:n_lat], kvg_ref[...]).astype(BF16)
        qt = lax.dot_general(wuqt_ref[...], cq, NT_DIMS, preferred_element_type=F32)
        kb = jnp.dot(ckv, wuk_ref[...], preferred_element_type=F32)
        vt = lax.dot_general(wuvt_ref[...], ckv, NT_DIMS, preferred_element_type=F32)

        cos_a, sin_a = swat_tab[0, :, rs], swat_tab[1, :, rs]
        for hd in range(SWA_HEADS):
            x1, x2 = qat[hd * HEAD_DIM:hd * HEAD_DIM + half], qat[hd * HEAD_DIM + half:(hd + 1) * HEAD_DIM]
            qat_ref[hd, :, rs] = jnp.concatenate(
                [x1 * cos_a - x2 * sin_a, x2 * cos_a + x1 * sin_a], axis=0).astype(BF16)
        ka_ref[rs, :] = _rope_lane_group(lat[:, n_lat:], swa_tab[0, rs, :], swa_tab[1, rs, :], half).astype(BF16)
        vat = p[:, o_v:o_v + LANES].T
        for g in range(SWA_KV_HEADS):
            for i in range(sub // BLOCK):
                vat_ref[g, t * (sub // BLOCK) + i] = jnp.concatenate(
                    [vat[g * HEAD_DIM:(g + 1) * HEAD_DIM, i * BLOCK:(i + 1) * BLOCK], ones[:, :BLOCK]],
                    axis=0).astype(BF16)

        cos_q, sin_q = mqt_tab[0, :, rs], mqt_tab[1, :, rs]
        for hd in range(MLA_HEADS):
            qh = qt[hd * MLA_QK_DIM:(hd + 1) * MLA_QK_DIM]
            x1, x2 = qh[r0:r1], qh[r1:r2]
            qt_ref[hd, :, rs] = jnp.concatenate(
                [qh[:r0] * MLA_Q_SCALE, x1 * cos_q - x2 * sin_q, x2 * cos_q + x1 * sin_q], axis=0).astype(BF16)

        kr = _rope_lane_group(p[:, o_kr:o_kr + LANES], mk_tab[0, rs, :], mk_tab[1, rs, :], MLA_ROPE_HALF)
        for hd in range(MLA_HEADS):
            km_ref[hd, rs, :] = (kb[:, hd * LANES:(hd + 1) * LANES] + kr).astype(BF16)
            vt_ref[hd, t] = jnp.concatenate(
                [vt[hd * MLA_V_DIM:(hd + 1) * MLA_V_DIM], ones], axis=0).astype(BF16)


def _in_proj(x, g, win, wqt, qg, wuqt, kvg, wuk, wuvt, swa_tab, swat_tab, mk_tab, mqt_tab):
    B, S, D = x.shape
    rows = PROJ_ROWS
    ns = S // rows
    kc = rows // MLA_K_ROWS
    H = MLA_HEADS
    G = SWA_KV_HEADS
    tok = lambda w: pl.BlockSpec((None, rows, w), lambda si, b: (b, si, 0))
    const = lambda a: pl.BlockSpec(a.shape, lambda si, b: (0,) * a.ndim)
    tab = pl.BlockSpec((2, rows, LANES), lambda si, b: (0, si, 0))
    tabt = lambda half: pl.BlockSpec((2, half, rows), lambda si, b: (0, 0, si))
    return pl.pallas_call(
        _in_proj_kernel,
        grid=(ns, B),
        in_specs=[tok(D), const(g), const(win), const(wqt), const(qg), const(wuqt), const(kvg), const(wuk),
                  const(wuvt), tab, tabt(HEAD_DIM // 2), tab, tabt(MLA_ROPE_HALF)],
        out_specs=[pl.BlockSpec((None, SWA_HEADS, HEAD_DIM, rows), lambda si, b: (b, 0, 0, si)),
                   tok(LANES),
                   pl.BlockSpec((None, G, rows // BLOCK, SWA_V_ROWS, BLOCK), lambda si, b: (b, 0, si, 0, 0)),
                   pl.BlockSpec((None, H, MLA_QK_DIM, rows), lambda si, b: (b, 0, 0, si)),
                   pl.BlockSpec((None, H, rows, LANES), lambda si, b: (b, 0, si, 0)),
                   pl.BlockSpec((None, H, kc, MLA_V_ROWS, MLA_K_ROWS), lambda si, b: (b, 0, si, 0, 0))],
        out_shape=[jax.ShapeDtypeStruct((B, SWA_HEADS, HEAD_DIM, S), BF16),
                   jax.ShapeDtypeStruct((B, S, LANES), BF16),
                   jax.ShapeDtypeStruct((B, G, S // BLOCK, SWA_V_ROWS, BLOCK), BF16),
                   jax.ShapeDtypeStruct((B, H, MLA_QK_DIM, S), BF16),
                   jax.ShapeDtypeStruct((B, H, S, LANES), BF16),
                   jax.ShapeDtypeStruct((B, H, S // MLA_K_ROWS, MLA_V_ROWS, MLA_K_ROWS), BF16)],
        compiler_params=pltpu.CompilerParams(
            dimension_semantics=("arbitrary", "arbitrary"), vmem_limit_bytes=VMEM_LIMIT_BYTES),
        name="in_proj",
    )(x, g, win, wqt, qg, wuqt, kvg, wuk, wuvt, swa_tab, swat_tab, mk_tab, mqt_tab)


def _swa_kernel(sink_ref, qt_ref, k_ref, vt_ref, *refs, n_cast):
    o_ref = refs[n_cast]
    for src, dst in zip(refs[:n_cast], refs[n_cast + 1:]):
        dst[...] = src[:, src.shape[1] - dst.shape[1]:].astype(BF16)
    n = pl.program_id(1)
    n_blk = SWA_Q_ROWS // BLOCK
    width = SWA_GROUP * BLOCK
    key = lax.broadcasted_iota(jnp.int32, (2 * BLOCK, width), 0)
    qry = lax.broadcasted_iota(jnp.int32, (2 * BLOCK, width), 1) & (BLOCK - 1)
    band = (key > qry) & (key <= qry + BLOCK)
    bias = jnp.where(band, 0.0, -jnp.inf).astype(F32)
    zeros = jnp.zeros((HEAD_DIM, width), BF16)

    def scores(unit):
        j, g = unit
        blk = n * n_blk + j
        prev0 = pl.multiple_of(jnp.maximum(blk - 1, 0) * BLOCK, BLOCK)
        cur0 = pl.multiple_of(blk * BLOCK, BLOCK)
        kw = jnp.concatenate([k_ref[pl.ds(prev0, BLOCK), :], k_ref[pl.ds(cur0, BLOCK), :]], axis=0)
        q = jnp.concatenate([qt_ref[SWA_GROUP * g + i, :, j * BLOCK:(j + 1) * BLOCK]
                             for i in range(SWA_GROUP)], axis=1)
        q = jnp.concatenate([q, zeros] if g == 0 else [zeros, q], axis=0)
        s = jnp.dot(kw, q, preferred_element_type=F32) + bias
        if j == 0:
            s = jnp.where((key >= BLOCK) | (blk > 0), s, -jnp.inf)
        return s

    def weighted_values(unit, p, m):
        j, g = unit
        blk = n * n_blk + j
        vw = jnp.concatenate([vt_ref[g, jnp.maximum(blk - 1, 0)], vt_ref[g, blk]], axis=1)
        o = jnp.dot(vw, p, preferred_element_type=F32)
        denom = o[HEAD_DIM:HEAD_DIM + 1] + jnp.exp2(sink_ref[g] - m)
        return o[:HEAD_DIM] / denom

    def store(j, outs):
        ot = jnp.concatenate([outs[g][:, i * BLOCK:(i + 1) * BLOCK]
                              for g in range(SWA_KV_HEADS) for i in range(SWA_GROUP)], axis=0)
        o_ref[j * BLOCK:(j + 1) * BLOCK, :] = ot.T.astype(BF16)

    def softmax(unit, s):
        m = jnp.maximum(jnp.max(s, axis=0, keepdims=True), sink_ref[unit[1]])
        return jnp.exp2(s - m).astype(BF16), m

    outs = {}

    def finish(unit, p, m):
        j, g = unit
        outs[g] = weighted_values(unit, p, m)
        if g == SWA_KV_HEADS - 1:
            store(j, outs)

    _software_pipeline([(j, g) for j in range(n_blk) for g in range(SWA_KV_HEADS)],
                       SWA_LOOKAHEAD, scores, softmax, finish)


def _swa_attention(sink_rows, qat, ka, vat, riders):
    B, H, _, S = qat.shape
    G = SWA_KV_HEADS
    rows = SWA_Q_ROWS
    per_batch = S // rows
    steps = B * per_batch

    def slab(n_rows, n_cols):
        assert n_rows % (steps * BF16_SUBLANES) == 0, n_rows
        return pl.BlockSpec((n_rows // steps, n_cols), lambda b, n: (b * per_batch + n, 0))

    out = pl.pallas_call(
        functools.partial(_swa_kernel, n_cast=len(riders)),
        grid=(B, per_batch),
        in_specs=[pl.BlockSpec(sink_rows.shape, lambda b, n: (0, 0, 0)),
                  pl.BlockSpec((None, H, HEAD_DIM, rows), lambda b, n: (b, 0, 0, n)),
                  pl.BlockSpec((None, S, LANES), lambda b, n: (b, 0, 0)),
                  pl.BlockSpec((None, G, S // BLOCK, SWA_V_ROWS, BLOCK), lambda b, n: (b, 0, 0, 0, 0))]
                 + [slab(*w.shape) for w, _ in riders],
        out_specs=[pl.BlockSpec((None, rows, H * HEAD_DIM), lambda b, n: (b, n, 0))]
                  + [slab(w.shape[0], keep) for w, keep in riders],
        out_shape=[jax.ShapeDtypeStruct((B, S, H * HEAD_DIM), BF16)]
                  + [jax.ShapeDtypeStruct((w.shape[0], keep), BF16) for w, keep in riders],
        compiler_params=pltpu.CompilerParams(
            dimension_semantics=("arbitrary", "arbitrary"), vmem_limit_bytes=VMEM_LIMIT_BYTES),
        name="swa_attention",
    )(sink_rows, qat, ka, vat, *[w for w, _ in riders])
    return out[0], out[1:]


def _mla_kernel(qt_ref, k_ref, vt_ref, o_ref, m_ref, acc_ref):
    n = pl.program_id(1)
    tq, tk = MLA_Q_COLS, MLA_K_ROWS
    per_tile = tq // tk

    m_ref[...] = jnp.full(m_ref.shape, -jnp.inf, F32)
    acc_ref[...] = jnp.zeros(acc_ref.shape, F32)
    causal_bias = jnp.where(lax.broadcasted_iota(jnp.int32, (tk, tq), 0)
                            <= lax.broadcasted_iota(jnp.int32, (tk, tq), 1), 0.0, -jnp.inf).astype(F32)

    def scores(unit):
        j, h, c0, masked = unit
        k0 = pl.multiple_of(j * tk, tk)
        s = jnp.dot(k_ref[h, pl.ds(k0, tk), :MLA_QK_DIM], qt_ref[h, :, c0:],
                    preferred_element_type=F32)
        if masked:
            s = s + causal_bias[:, :tq - c0]
        return s

    def weighted_values(unit, p, alpha):
        j, h, c0, masked = unit
        acc_ref[h, :, c0:] = alpha * acc_ref[h, :, c0:] + jnp.dot(vt_ref[h, j], p, preferred_element_type=F32)
        if masked and c0 == (per_tile - 1) * tk and h % 2 == 1:
            pair = jnp.concatenate([acc_ref[i, :MLA_V_DIM] / acc_ref[i, MLA_V_DIM:MLA_V_DIM + 1]
                                    for i in (h - 1, h)], axis=0)
            o_ref[:, (h - 1) * MLA_V_DIM:(h + 1) * MLA_V_DIM] = pair.astype(BF16).T

    def softmax(unit, s):
        _, h, c0, _ = unit
        m_prev = m_ref[h, :, c0:]
        m_new = jnp.maximum(m_prev, jnp.max(s, axis=0, keepdims=True))
        m_ref[h, :, c0:] = m_new
        return jnp.exp2(s - m_new).astype(BF16), jnp.exp2(m_prev - m_new)

    def run(units):
        _software_pipeline(units, MLA_LOOKAHEAD, scores, softmax, weighted_values)

    def below(i):
        return [(i * per_tile + d, h, 0, False) for d in range(per_tile) for h in range(MLA_HEADS)]

    diagonal = [(n * per_tile + d, h, d * tk, True) for d in range(per_tile) for h in range(MLA_HEADS)]

    trips = jnp.maximum(n - 1, 0) // 2

    def body(i, carry):
        run(below(2 * i) + below(2 * i + 1))
        return carry

    lax.fori_loop(0, trips, body, 0)
    for left in range(3):
        @pl.when(n - 2 * trips == left)
        def _():
            run([u for r in range(left) for u in below(2 * trips + r)] + diagonal)


def _mla_attention(qt, km, vt):
    B, H, _, S = qt.shape
    tq, tk = MLA_Q_COLS, MLA_K_ROWS
    assert tq % tk == 0
    return pl.pallas_call(
        _mla_kernel,
        grid=(B, S // tq),
        in_specs=[pl.BlockSpec((None, H, MLA_QK_DIM, tq), lambda b, n: (b, 0, 0, n)),
                  pl.BlockSpec((None, H, S, LANES), lambda b, n: (b, 0, 0, 0)),
                  pl.BlockSpec((None, H, S // tk, MLA_V_ROWS, tk), lambda b, n: (b, 0, 0, 0, 0))],
        out_specs=pl.BlockSpec((None, tq, H * MLA_V_DIM), lambda b, n: (b, n, 0)),
        out_shape=jax.ShapeDtypeStruct((B, S, H * MLA_V_DIM), BF16),
        scratch_shapes=[pltpu.VMEM((H, 1, tq), F32),
                        pltpu.VMEM((H, MLA_V_ROWS, tq), F32)],
        compiler_params=pltpu.CompilerParams(
            dimension_semantics=("arbitrary", "arbitrary"), vmem_limit_bytes=VMEM_LIMIT_BYTES),
        name="mla_attention",
    )(qt, km, vt)


def _post_kernel(x_ref, oa_ref, ob_ref, g1_ref, wg_ref, woa_ref, wob_ref, wout_ref,
                 g2_ref, wgate_ref, wup_ref, wdown_ref, g3_ref, out_ref, *, final_norm):
    D = x_ref.shape[-1]
    dot = functools.partial(jnp.dot, preferred_element_type=F32)
    n_sub = x_ref.shape[0] // POST_SUB_ROWS
    for t0 in range(0, n_sub, POST_INTERLEAVE):
        tiles = [slice((t0 + i) * POST_SUB_ROWS, (t0 + i + 1) * POST_SUB_ROWS) for i in range(POST_INTERLEAVE)]
        xs = [x_ref[rs, :] for rs in tiles]
        hs = [_rmsnorm_f32(x, g1_ref[...]).astype(BF16) for x in xs]
        gates = [dot(h, wg_ref[...]) for h in hs]
        yas = [dot(oa_ref[rs, :], woa_ref[...]) for rs in tiles]
        ybs = [dot(ob_ref[rs, :], wob_ref[...]) for rs in tiles]
        ys = [(jax.nn.sigmoid(g[:, :D]) * ya + jax.nn.sigmoid(g[:, D:]) * yb).astype(BF16)
              for g, ya, yb in zip(gates, yas, ybs)]
        x1s = [x + dot(y, wout_ref[...]) for x, y in zip(xs, ys)]

        h2s = [_rmsnorm_f32(x1, g2_ref[...]).astype(BF16) for x1 in x1s]
        hgs = [dot(h2, wgate_ref[...]) for h2 in h2s]
        hus = [dot(h2, wup_ref[...]) for h2 in h2s]
        acts = [(hg * jax.nn.sigmoid(hg) * hu).astype(BF16) for hg, hu in zip(hgs, hus)]
        x2s = [x1 + dot(a, wdown_ref[...]) for x1, a in zip(x1s, acts)]
        for rs, x2 in zip(tiles, x2s):
            out_ref[rs, :] = _rmsnorm_f32(x2, g3_ref[...]) if final_norm else x2


def _post(x, oa, ob, g1, wg, woa, wob, wout, g2, wgate, wup, wdown, g3, final_norm):
    B, S, D = x.shape
    rows = POST_ROWS
    tok = lambda w: pl.BlockSpec((None, rows, w), lambda b, i: (b, i, 0))
    const = lambda a: pl.BlockSpec(a.shape, lambda b, i: (0,) * a.ndim, pipeline_mode=pl.Buffered(1))
    return pl.pallas_call(
        functools.partial(_post_kernel, final_norm=final_norm),
        grid=(B, S // rows),
        in_specs=[tok(D), tok(oa.shape[-1]), tok(ob.shape[-1])]
                 + [const(a) for a in (g1, wg, woa, wob, wout, g2, wgate, wup, wdown, g3)],
        out_specs=tok(D),
        out_shape=jax.ShapeDtypeStruct((B, S, D), F32),
        compiler_params=pltpu.CompilerParams(
            dimension_semantics=("arbitrary", "arbitrary"), vmem_limit_bytes=VMEM_LIMIT_BYTES),
        name="post_attention",
    )(x, oa, ob, g1, wg, woa, wob, wout, g2, wgate, wup, wdown, g3)


def _rope_angles_t(seq, dim):
    inv = ROPE_THETA ** (-jnp.arange(0, dim, 2, dtype=F32) / dim)
    ang = inv[:, None] * jnp.arange(seq, dtype=F32)[None, :]
    return jnp.cos(ang), jnp.sin(ang)


def _rope_table(cos_t, sin_t, lead, trail):
    cos, sin = cos_t.T, sin_t.T
    seq, half = cos.shape
    reps = (LANES - lead - trail) // (2 * half)
    c = [jnp.ones((seq, lead), F32)] + [cos] * (2 * reps) + [jnp.ones((seq, trail), F32)]
    s = [jnp.zeros((seq, lead), F32)] + [-sin, sin] * reps + [jnp.zeros((seq, trail), F32)]
    return jnp.stack([jnp.concatenate(c, axis=1), jnp.concatenate(s, axis=1)])


def kernel(x, mix_norm_g, w_in, swa_sinks, q_norm_g, w_uq, kv_norm_g, w_ukv, w_o_swa, w_o_mla,
           w_out, ffn_norm_g, w_gate, w_up, w_down, final_norm_g):
    B, S, D = x.shape
    depth = w_in.shape[0]
    assert S % SWA_Q_ROWS == 0 and S % PROJ_ROWS == 0 and S % POST_ROWS == 0 and S % MLA_Q_COLS == 0
    assert PROJ_ROWS % MLA_K_ROWS == 0

    cos_a, sin_a, cos_m, sin_m = lax.optimization_barrier(
        _rope_angles_t(S, HEAD_DIM) + _rope_angles_t(S, MLA_ROPE_DIM))
    swa_tab = _rope_table(cos_a, sin_a, 0, 0)
    mk_tab = _rope_table(cos_m, sin_m, MLA_NOPE_DIM, LANES - MLA_QK_DIM)
    swat_tab = jnp.stack([cos_a, sin_a]) * SWA_Q_SCALE
    mqt_tab = jnp.stack([cos_m, sin_m]) * MLA_Q_SCALE
    row = lambda v: v.reshape(1, -1)

    for l in range(depth):
        wi = w_in[l]
        wkr = jnp.pad(wi[:, 1408:1440], ((0, 0), (MLA_NOPE_DIM, LANES - MLA_QK_DIM)))
        win = jnp.concatenate([wi[:, 768:1408], wi[:, 512:768], wkr], axis=1).astype(BF16)
        wqt = wi[:, :512].T.astype(BF16)
        sink_rows = jnp.repeat(swa_sinks[l].reshape(SWA_KV_HEADS, 1, SWA_GROUP) * LOG2_E, BLOCK, axis=2)
        wuqt = w_uq[l].T.astype(BF16)
        wkv = w_ukv[l].reshape(KV_LORA_RANK, MLA_HEADS, MLA_NOPE_DIM + MLA_V_DIM)
        wuk = jnp.pad(wkv[:, :, :MLA_NOPE_DIM],
                      ((0, 0), (0, 0), (0, LANES - MLA_NOPE_DIM))).reshape(KV_LORA_RANK, -1).astype(BF16)
        wuvt = wkv[:, :, MLA_NOPE_DIM:].reshape(KV_LORA_RANK, -1).T.astype(BF16)

        qat, ka, vat, qt, km, vt = _in_proj(x, row(mix_norm_g[l]), win, wqt, row(q_norm_g[l]), wuqt,
                                            row(kv_norm_g[l]), wuk, wuvt, swa_tab, swat_tab, mk_tab, mqt_tab)
        whole = lambda w: (w, w.shape[1])
        oa, (wg, woa, wob, wout, wgate, wup, wdown) = _swa_attention(
            sink_rows, qat, ka, vat,
            [(wi, 2 * D)] + [whole(w[l]) for w in (w_o_swa, w_o_mla, w_out, w_gate, w_up, w_down)])
        ob = _mla_attention(qt, km, vt)
        x = _post(x, oa, ob, row(mix_norm_g[l]), wg, woa, wob, wout, row(ffn_norm_g[l]), wgate, wup, wdown,
                  row(final_norm_g), final_norm=(l == depth - 1))
    return x
```

```python
import functools
import math

import jax
import jax.numpy as jnp
from jax import lax
from jax.experimental import pallas as pl
from jax.experimental.pallas import tpu as pltpu

EPS = 1e-6
ROPE_THETA = 10000.0
BLOCK = 128
HEAD_DIM = 64
SWA_HEADS = 8
SWA_KV_HEADS = 2
SWA_GROUP = SWA_HEADS // SWA_KV_HEADS
MLA_HEADS = 8
MLA_NOPE_DIM = 64
MLA_ROPE_DIM = 32
MLA_V_DIM = 64
MLA_QK_DIM = MLA_NOPE_DIM + MLA_ROPE_DIM
Q_LORA_RANK = 384
KV_LORA_RANK = 256

LANES = 128
BF16_SUBLANES = 16
V7X_VMEM_BYTES = 64 * 1024 * 1024
VMEM_LIMIT_BYTES = 56 * 1024 * 1024

PROJ_ROWS = 1024
SWA_Q_ROWS = 2048
MLA_Q_COLS = 512
MLA_K_ROWS = 256
POST_ROWS = 1024
POST_SUB_ROWS = 256
POST_INTERLEAVE = 2
MLA_LOOKAHEAD = 2
SWA_LOOKAHEAD = 2

MLA_ROPE_HALF = MLA_ROPE_DIM // 2
MLA_V_ROWS = MLA_V_DIM + 16
SWA_V_ROWS = HEAD_DIM + 16
LOG2_E = math.log2(math.e)
MLA_Q_SCALE = (MLA_QK_DIM ** -0.5) * LOG2_E
SWA_Q_SCALE = (HEAD_DIM ** -0.5) * LOG2_E

F32 = jnp.float32
BF16 = jnp.bfloat16
NT_DIMS = (((1,), (1,)), ((), ()))


def _rmsnorm_f32(x, g):
    return x * lax.rsqrt(jnp.mean(x * x, axis=-1, keepdims=True) + EPS) * g


def _software_pipeline(units, lookahead, first, middle, last):
    ahead = {i: first(units[i]) for i in range(min(lookahead, len(units)))}
    pending = None
    for i, unit in enumerate(units):
        if i + lookahead < len(units):
            ahead[i + lookahead] = first(units[i + lookahead])
        mid = middle(unit, ahead.pop(i))
        if pending is not None:
            last(pending[0], *pending[1])
        pending = (unit, mid)
    last(pending[0], *pending[1])


def _rope_lane_group(x, cos, sin_signed, half):
    up = pltpu.roll(x, LANES - half, axis=1)
    down = pltpu.roll(x, half, axis=1)
    first = (lax.broadcasted_iota(jnp.int32, x.shape, 1) & (2 * half - 1)) < half
    return x * cos + jnp.where(first, up, down) * sin_signed


def _in_proj_kernel(x_ref, g_ref, win_ref, wqt_ref, qg_ref, wuqt_ref, kvg_ref, wuk_ref, wuvt_ref,
                    swa_tab, swat_tab, mk_tab, mqt_tab,
                    qat_ref, ka_ref, vat_ref, qt_ref, km_ref, vt_ref):
    n_lat = Q_LORA_RANK + KV_LORA_RANK
    n_first = n_lat + LANES
    o_v, o_kr = 0, LANES
    sub = MLA_K_ROWS
    ones = jnp.ones((MLA_V_ROWS - MLA_V_DIM, sub), F32)
    q_pad = jnp.zeros((LANES - MLA_QK_DIM, sub), F32)
    r0, r1, r2 = MLA_NOPE_DIM, MLA_NOPE_DIM + MLA_ROPE_HALF, MLA_QK_DIM
    half = HEAD_DIM // 2

    for t in range(x_ref.shape[0] // sub):
        rs = slice(t * sub, (t + 1) * sub)
        h = _rmsnorm_f32(x_ref[rs, :], g_ref[...]).astype(BF16)
        lat = jnp.dot(h, win_ref[:, :n_first], preferred_element_type=F32)
        qat = lax.dot_general(wqt_ref[...], h, NT_DIMS, preferred_element_type=F32)
        p = jnp.dot(h, win_ref[:, n_first:], preferred_element_type=F32)
        cq = _rmsnorm_f32(lat[:, :Q_LORA_RANK], qg_ref[...]).astype(BF16)
        ckv = _rmsnorm_f32(lat[:, Q_LORA_RANK:n_lat], kvg_ref[...]).astype(BF16)
        qt = lax.dot_general(wuqt_ref[...], cq, NT_DIMS, preferred_element_type=F32)
        kb = jnp.dot(ckv, wuk_ref[...], preferred_element_type=F32)
        vt = lax.dot_general(wuvt_ref[...], ckv, NT_DIMS, preferred_element_type=F32)

        cos_a, sin_a = swat_tab[0, :, rs], swat_tab[1, :, rs]
        for hd in range(SWA_HEADS):
            x1, x2 = qat[hd * HEAD_DIM:hd * HEAD_DIM + half], qat[hd * HEAD_DIM + half:(hd + 1) * HEAD_DIM]
            qat_ref[hd, :, rs] = jnp.concatenate(
                [x1 * cos_a - x2 * sin_a, x2 * cos_a + x1 * sin_a], axis=0).astype(BF16)
        ka_ref[rs, :] = _rope_lane_group(lat[:, n_lat:], swa_tab[0, rs, :], swa_tab[1, rs, :], half).astype(BF16)
        vat = p[:, o_v:o_v + LANES].T
        for g in range(SWA_KV_HEADS):
            for i in range(sub // BLOCK):
                vat_ref[g, t * (sub // BLOCK) + i] = jnp.concatenate(
                    [vat[g * HEAD_DIM:(g + 1) * HEAD_DIM, i * BLOCK:(i + 1) * BLOCK], ones[:, :BLOCK]],
                    axis=0).astype(BF16)

        cos_q, sin_q = mqt_tab[0, :, rs], mqt_tab[1, :, rs]
        for hd in range(MLA_HEADS):
            qh = qt[hd * MLA_QK_DIM:(hd + 1) * MLA_QK_DIM]
            x1, x2 = qh[r0:r1], qh[r1:r2]
            qt_ref[hd, :, rs] = jnp.concatenate(
                [qh[:r0] * MLA_Q_SCALE, x1 * cos_q - x2 * sin_q, x2 * cos_q + x1 * sin_q, q_pad],
                axis=0).astype(BF16)

        kr = _rope_lane_group(p[:, o_kr:o_kr + LANES], mk_tab[0, rs, :], mk_tab[1, rs, :], MLA_ROPE_HALF)
        nope_lanes = lax.broadcasted_iota(jnp.int32, kr.shape, 1) < MLA_NOPE_DIM
        for hd in range(MLA_HEADS):
            pair = kb[:, (hd // 2) * LANES:(hd // 2 + 1) * LANES]
            if hd % 2:
                pair = pltpu.roll(pair, MLA_NOPE_DIM, axis=1)
            km_ref[hd, rs, :] = jnp.where(nope_lanes, pair, kr).astype(BF16)
            vt_ref[hd, t] = jnp.concatenate(
                [vt[hd * MLA_V_DIM:(hd + 1) * MLA_V_DIM], ones], axis=0).astype(BF16)


def _in_proj(x, g, win, wqt, qg, wuqt, kvg, wuk, wuvt, swa_tab, swat_tab, mk_tab, mqt_tab):
    B, S, D = x.shape
    rows = PROJ_ROWS
    ns = S // rows
    kc = rows // MLA_K_ROWS
    H = MLA_HEADS
    G = SWA_KV_HEADS
    tok = lambda w: pl.BlockSpec((None, rows, w), lambda si, b: (b, si, 0))
    const = lambda a: pl.BlockSpec(a.shape, lambda si, b: (0,) * a.ndim)
    tab = pl.BlockSpec((2, rows, LANES), lambda si, b: (0, si, 0))
    tabt = lambda half: pl.BlockSpec((2, half, rows), lambda si, b: (0, 0, si))
    return pl.pallas_call(
        _in_proj_kernel,
        grid=(ns, B),
        in_specs=[tok(D), const(g), const(win), const(wqt), const(qg), const(wuqt), const(kvg), const(wuk),
                  const(wuvt), tab, tabt(HEAD_DIM // 2), tab, tabt(MLA_ROPE_HALF)],
        out_specs=[pl.BlockSpec((None, SWA_HEADS, HEAD_DIM, rows), lambda si, b: (b, 0, 0, si)),
                   tok(LANES),
                   pl.BlockSpec((None, G, rows // BLOCK, SWA_V_ROWS, BLOCK), lambda si, b: (b, 0, si, 0, 0)),
                   pl.BlockSpec((None, H, LANES, rows), lambda si, b: (b, 0, 0, si)),
                   pl.BlockSpec((None, H, rows, LANES), lambda si, b: (b, 0, si, 0)),
                   pl.BlockSpec((None, H, kc, MLA_V_ROWS, MLA_K_ROWS), lambda si, b: (b, 0, si, 0, 0))],
        out_shape=[jax.ShapeDtypeStruct((B, SWA_HEADS, HEAD_DIM, S), BF16),
                   jax.ShapeDtypeStruct((B, S, LANES), BF16),
                   jax.ShapeDtypeStruct((B, G, S // BLOCK, SWA_V_ROWS, BLOCK), BF16),
                   jax.ShapeDtypeStruct((B, H, LANES, S), BF16),
                   jax.ShapeDtypeStruct((B, H, S, LANES), BF16),
                   jax.ShapeDtypeStruct((B, H, S // MLA_K_ROWS, MLA_V_ROWS, MLA_K_ROWS), BF16)],
        compiler_params=pltpu.CompilerParams(
            dimension_semantics=("arbitrary", "arbitrary"), vmem_limit_bytes=VMEM_LIMIT_BYTES),
        name="in_proj",
    )(x, g, win, wqt, qg, wuqt, kvg, wuk, wuvt, swa_tab, swat_tab, mk_tab, mqt_tab)


def _swa_kernel(sink_ref, qt_ref, k_ref, vt_ref, *refs, n_cast):
    o_ref = refs[n_cast]
    for src, dst in zip(refs[:n_cast], refs[n_cast + 1:]):
        dst[...] = src[:, src.shape[1] - dst.shape[1]:].astype(BF16)
    n = pl.program_id(1)
    n_blk = SWA_Q_ROWS // BLOCK
    width = SWA_GROUP * BLOCK
    key = lax.broadcasted_iota(jnp.int32, (2 * BLOCK, width), 0)
    qry = lax.broadcasted_iota(jnp.int32, (2 * BLOCK, width), 1) & (BLOCK - 1)
    band = (key > qry) & (key <= qry + BLOCK)
    bias = jnp.where(band, 0.0, -jnp.inf).astype(F32)
    zeros = jnp.zeros((HEAD_DIM, width), BF16)

    def scores(unit):
        j, g = unit
        blk = n * n_blk + j
        prev0 = pl.multiple_of(jnp.maximum(blk - 1, 0) * BLOCK, BLOCK)
        cur0 = pl.multiple_of(blk * BLOCK, BLOCK)
        kw = jnp.concatenate([k_ref[pl.ds(prev0, BLOCK), :], k_ref[pl.ds(cur0, BLOCK), :]], axis=0)
        q = jnp.concatenate([qt_ref[SWA_GROUP * g + i, :, j * BLOCK:(j + 1) * BLOCK]
                             for i in range(SWA_GROUP)], axis=1)
        q = jnp.concatenate([q, zeros] if g == 0 else [zeros, q], axis=0)
        s = jnp.dot(kw, q, preferred_element_type=F32) + bias
        if j == 0:
            s = jnp.where((key >= BLOCK) | (blk > 0), s, -jnp.inf)
        return s

    def weighted_values(unit, p, m):
        j, g = unit
        blk = n * n_blk + j
        vw = jnp.concatenate([vt_ref[g, jnp.maximum(blk - 1, 0)], vt_ref[g, blk]], axis=1)
        o = jnp.dot(vw, p, preferred_element_type=F32)
        denom = o[HEAD_DIM:HEAD_DIM + 1] + jnp.exp2(sink_ref[g] - m)
        return o[:HEAD_DIM] / denom

    def store(j, outs):
        ot = jnp.concatenate([outs[g][:, i * BLOCK:(i + 1) * BLOCK]
                              for g in range(SWA_KV_HEADS) for i in range(SWA_GROUP)], axis=0)
        o_ref[j * BLOCK:(j + 1) * BLOCK, :] = ot.T.astype(BF16)

    def softmax(unit, s):
        m = jnp.maximum(jnp.max(s, axis=0, keepdims=True), sink_ref[unit[1]])
        return jnp.exp2(s - m).astype(BF16), m

    outs = {}

    def finish(unit, p, m):
        j, g = unit
        outs[g] = weighted_values(unit, p, m)
        if g == SWA_KV_HEADS - 1:
            store(j, outs)

    _software_pipeline([(j, g) for j in range(n_blk) for g in range(SWA_KV_HEADS)],
                       SWA_LOOKAHEAD, scores, softmax, finish)


def _swa_attention(sink_rows, qat, ka, vat, riders):
    B, H, _, S = qat.shape
    G = SWA_KV_HEADS
    rows = SWA_Q_ROWS
    per_batch = S // rows
    steps = B * per_batch

    def slab(n_rows, n_cols):
        assert n_rows % (steps * BF16_SUBLANES) == 0, n_rows
        return pl.BlockSpec((n_rows // steps, n_cols), lambda b, n: (b * per_batch + n, 0))

    out = pl.pallas_call(
        functools.partial(_swa_kernel, n_cast=len(riders)),
        grid=(B, per_batch),
        in_specs=[pl.BlockSpec(sink_rows.shape, lambda b, n: (0, 0, 0)),
                  pl.BlockSpec((None, H, HEAD_DIM, rows), lambda b, n: (b, 0, 0, n)),
                  pl.BlockSpec((None, S, LANES), lambda b, n: (b, 0, 0)),
                  pl.BlockSpec((None, G, S // BLOCK, SWA_V_ROWS, BLOCK), lambda b, n: (b, 0, 0, 0, 0))]
                 + [slab(*w.shape) for w, _ in riders],
        out_specs=[pl.BlockSpec((None, rows, H * HEAD_DIM), lambda b, n: (b, n, 0))]
                  + [slab(w.shape[0], keep) for w, keep in riders],
        out_shape=[jax.ShapeDtypeStruct((B, S, H * HEAD_DIM), BF16)]
                  + [jax.ShapeDtypeStruct((w.shape[0], keep), BF16) for w, keep in riders],
        compiler_params=pltpu.CompilerParams(
            dimension_semantics=("arbitrary", "arbitrary"), vmem_limit_bytes=VMEM_LIMIT_BYTES),
        name="swa_attention",
    )(sink_rows, qat, ka, vat, *[w for w, _ in riders])
    return out[0], out[1:]


def _mla_kernel(qt_ref, k_ref, vt_ref, o_ref, m_ref, acc_ref):
    n = pl.program_id(1)
    tq, tk = MLA_Q_COLS, MLA_K_ROWS
    per_tile = tq // tk

    m_ref[...] = jnp.full(m_ref.shape, -jnp.inf, F32)
    acc_ref[...] = jnp.zeros(acc_ref.shape, F32)
    causal_bias = jnp.where(lax.broadcasted_iota(jnp.int32, (tk, tq), 0)
                            <= lax.broadcasted_iota(jnp.int32, (tk, tq), 1), 0.0, -jnp.inf).astype(F32)

    def scores(unit):
        j, h, c0, masked = unit
        k0 = pl.multiple_of(j * tk, tk)
        s = jnp.dot(k_ref[h, pl.ds(k0, tk), :], qt_ref[h, :, c0:], preferred_element_type=F32)
        if masked:
            s = s + causal_bias[:, :tq - c0]
        return s

    def weighted_values(unit, p, alpha):
        j, h, c0, masked = unit
        acc_ref[h, :, c0:] = alpha * acc_ref[h, :, c0:] + jnp.dot(vt_ref[h, j], p, preferred_element_type=F32)
        if masked and c0 == (per_tile - 1) * tk and h % 2 == 1:
            pair = jnp.concatenate([acc_ref[i, :MLA_V_DIM] / acc_ref[i, MLA_V_DIM:MLA_V_DIM + 1]
                                    for i in (h - 1, h)], axis=0)
            o_ref[:, (h - 1) * MLA_V_DIM:(h + 1) * MLA_V_DIM] = pair.astype(BF16).T

    def softmax(unit, s):
        _, h, c0, _ = unit
        m_prev = m_ref[h, :, c0:]
        m_new = jnp.maximum(m_prev, jnp.max(s, axis=0, keepdims=True))
        m_ref[h, :, c0:] = m_new
        return jnp.exp2(s - m_new).astype(BF16), jnp.exp2(m_prev - m_new)

    def run(units):
        _software_pipeline(units, MLA_LOOKAHEAD, scores, softmax, weighted_values)

    def below(i):
        return [(i * per_tile + d, h, 0, False) for d in range(per_tile) for h in range(MLA_HEADS)]

    diagonal = [(n * per_tile + d, h, d * tk, True) for d in range(per_tile) for h in range(MLA_HEADS)]

    trips = jnp.maximum(n - 1, 0) // 2

    def body(i, carry):
        run(below(2 * i) + below(2 * i + 1))
        return carry

    lax.fori_loop(0, trips, body, 0)
    for left in range(3):
        @pl.when(n - 2 * trips == left)
        def _():
            run([u for r in range(left) for u in below(2 * trips + r)] + diagonal)


def _mla_attention(qt, km, vt):
    B, H, _, S = qt.shape
    tq, tk = MLA_Q_COLS, MLA_K_ROWS
    assert tq % tk == 0
    return pl.pallas_call(
        _mla_kernel,
        grid=(B, S // tq),
        in_specs=[pl.BlockSpec((None, H, LANES, tq), lambda b, n: (b, 0, 0, n)),
                  pl.BlockSpec((None, H, S, LANES), lambda b, n: (b, 0, 0, 0)),
                  pl.BlockSpec((None, H, S // tk, MLA_V_ROWS, tk), lambda b, n: (b, 0, 0, 0, 0))],
        out_specs=pl.BlockSpec((None, tq, H * MLA_V_DIM), lambda b, n: (b, n, 0)),
        out_shape=jax.ShapeDtypeStruct((B, S, H * MLA_V_DIM), BF16),
        scratch_shapes=[pltpu.VMEM((H, 1, tq), F32),
                        pltpu.VMEM((H, MLA_V_ROWS, tq), F32)],
        compiler_params=pltpu.CompilerParams(
            dimension_semantics=("arbitrary", "arbitrary"), vmem_limit_bytes=VMEM_LIMIT_BYTES),
        name="mla_attention",
    )(qt, km, vt)


def _post_kernel(x_ref, oa_ref, ob_ref, g1_ref, wg_ref, woa_ref, wob_ref, wout_ref,
                 g2_ref, wgate_ref, wup_ref, wdown_ref, g3_ref, out_ref, *, final_norm):
    D = x_ref.shape[-1]
    dot = functools.partial(jnp.dot, preferred_element_type=F32)
    n_sub = x_ref.shape[0] // POST_SUB_ROWS
    for t0 in range(0, n_sub, POST_INTERLEAVE):
        tiles = [slice((t0 + i) * POST_SUB_ROWS, (t0 + i + 1) * POST_SUB_ROWS) for i in range(POST_INTERLEAVE)]
        xs = [x_ref[rs, :] for rs in tiles]
        hs = [_rmsnorm_f32(x, g1_ref[...]).astype(BF16) for x in xs]
        gates = [dot(h, wg_ref[...]) for h in hs]
        yas = [dot(oa_ref[rs, :], woa_ref[...]) for rs in tiles]
        ybs = [dot(ob_ref[rs, :], wob_ref[...]) for rs in tiles]
        ys = [(jax.nn.sigmoid(g[:, :D]) * ya + jax.nn.sigmoid(g[:, D:]) * yb).astype(BF16)
              for g, ya, yb in zip(gates, yas, ybs)]
        x1s = [x + dot(y, wout_ref[...]) for x, y in zip(xs, ys)]

        h2s = [_rmsnorm_f32(x1, g2_ref[...]).astype(BF16) for x1 in x1s]
        hgs = [dot(h2, wgate_ref[...]) for h2 in h2s]
        hus = [dot(h2, wup_ref[...]) for h2 in h2s]
        acts = [(hg * jax.nn.sigmoid(hg) * hu).astype(BF16) for hg, hu in zip(hgs, hus)]
        x2s = [x1 + dot(a, wdown_ref[...]) for x1, a in zip(x1s, acts)]
        for rs, x2 in zip(tiles, x2s):
            out_ref[rs, :] = _rmsnorm_f32(x2, g3_ref[...]) if final_norm else x2


def _post(x, oa, ob, g1, wg, woa, wob, wout, g2, wgate, wup, wdown, g3, final_norm):
    B, S, D = x.shape
    rows = POST_ROWS
    tok = lambda w: pl.BlockSpec((None, rows, w), lambda b, i: (b, i, 0))
    const = lambda a: pl.BlockSpec(a.shape, lambda b, i: (0,) * a.ndim, pipeline_mode=pl.Buffered(1))
    return pl.pallas_call(
        functools.partial(_post_kernel, final_norm=final_norm),
        grid=(B, S // rows),
        in_specs=[tok(D), tok(oa.shape[-1]), tok(ob.shape[-1])]
                 + [const(a) for a in (g1, wg, woa, wob, wout, g2, wgate, wup, wdown, g3)],
        out_specs=tok(D),
        out_shape=jax.ShapeDtypeStruct((B, S, D), F32),
        compiler_params=pltpu.CompilerParams(
            dimension_semantics=("arbitrary", "arbitrary"), vmem_limit_bytes=VMEM_LIMIT_BYTES),
        name="post_attention",
    )(x, oa, ob, g1, wg, woa, wob, wout, g2, wgate, wup, wdown, g3)


def _rope_angles_t(seq, dim):
    inv = ROPE_THETA ** (-jnp.arange(0, dim, 2, dtype=F32) / dim)
    ang = inv[:, None] * jnp.arange(seq, dtype=F32)[None, :]
    return jnp.cos(ang), jnp.sin(ang)


def _rope_table(cos_t, sin_t, lead, trail):
    cos, sin = cos_t.T, sin_t.T
    seq, half = cos.shape
    reps = (LANES - lead - trail) // (2 * half)
    c = [jnp.ones((seq, lead), F32)] + [cos] * (2 * reps) + [jnp.ones((seq, trail), F32)]
    s = [jnp.zeros((seq, lead), F32)] + [-sin, sin] * reps + [jnp.zeros((seq, trail), F32)]
    return jnp.stack([jnp.concatenate(c, axis=1), jnp.concatenate(s, axis=1)])


def kernel(x, mix_norm_g, w_in, swa_sinks, q_norm_g, w_uq, kv_norm_g, w_ukv, w_o_swa, w_o_mla,
           w_out, ffn_norm_g, w_gate, w_up, w_down, final_norm_g):
    B, S, D = x.shape
    depth = w_in.shape[0]
    assert S % SWA_Q_ROWS == 0 and S % PROJ_ROWS == 0 and S % POST_ROWS == 0 and S % MLA_Q_COLS == 0
    assert PROJ_ROWS % MLA_K_ROWS == 0

    cos_a, sin_a, cos_m, sin_m = lax.optimization_barrier(
        _rope_angles_t(S, HEAD_DIM) + _rope_angles_t(S, MLA_ROPE_DIM))
    swa_tab = _rope_table(cos_a, sin_a, 0, 0)
    mk_tab = _rope_table(cos_m, sin_m, MLA_NOPE_DIM, LANES - MLA_QK_DIM)
    swat_tab = jnp.stack([cos_a, sin_a]) * SWA_Q_SCALE
    mqt_tab = jnp.stack([cos_m, sin_m]) * MLA_Q_SCALE
    row = lambda v: v.reshape(1, -1)

    for l in range(depth):
        wi = w_in[l]
        wkr = jnp.pad(wi[:, 1408:1440], ((0, 0), (MLA_NOPE_DIM, LANES - MLA_QK_DIM)))
        win = jnp.concatenate([wi[:, 768:1408], wi[:, 512:768], wkr], axis=1).astype(BF16)
        wqt = wi[:, :512].T.astype(BF16)
        wg = wi[:, 1440:].astype(BF16)
        sink_rows = jnp.repeat(swa_sinks[l].reshape(SWA_KV_HEADS, 1, SWA_GROUP) * LOG2_E, BLOCK, axis=2)
        wuqt = w_uq[l].T.astype(BF16)
        wkv = w_ukv[l].reshape(KV_LORA_RANK, MLA_HEADS, MLA_NOPE_DIM + MLA_V_DIM)
        wuk = wkv[:, :, :MLA_NOPE_DIM].reshape(KV_LORA_RANK, -1).astype(BF16)
        wuvt = wkv[:, :, MLA_NOPE_DIM:].reshape(KV_LORA_RANK, -1).T.astype(BF16)

        qat, ka, vat, qt, km, vt = _in_proj(x, row(mix_norm_g[l]), win, wqt, row(q_norm_g[l]), wuqt,
                                            row(kv_norm_g[l]), wuk, wuvt, swa_tab, swat_tab, mk_tab, mqt_tab)
        whole = lambda w: (w, w.shape[1])
        oa, (woa, wob, wout, wgate, wup, wdown) = _swa_attention(
            sink_rows, qat, ka, vat, [whole(w[l]) for w in (w_o_swa, w_o_mla, w_out, w_gate, w_up, w_down)])
        ob = _mla_attention(qt, km, vt)
        x = _post(x, oa, ob, row(mix_norm_g[l]), wg, woa, wob, wout, row(ffn_norm_g[l]), wgate, wup, wdown,
                  row(final_norm_g), final_norm=(l == depth - 1))
    return x
```

```python
import functools
import math

import jax
import jax.numpy as jnp
from jax import lax
from jax.experimental import pallas as pl
from jax.experimental.pallas import tpu as pltpu

EPS = 1e-6
ROPE_THETA = 10000.0
BLOCK = 128
HEAD_DIM = 64
SWA_HEADS = 8
SWA_KV_HEADS = 2
SWA_GROUP = SWA_HEADS // SWA_KV_HEADS
MLA_HEADS = 8
MLA_NOPE_DIM = 64
MLA_ROPE_DIM = 32
MLA_V_DIM = 64
MLA_QK_DIM = MLA_NOPE_DIM + MLA_ROPE_DIM
Q_LORA_RANK = 384
KV_LORA_RANK = 256

LANES = 128
BF16_SUBLANES = 16
V7X_VMEM_BYTES = 64 * 1024 * 1024
VMEM_LIMIT_BYTES = 56 * 1024 * 1024

PROJ_ROWS = 1024
PROJ_INTERLEAVE = 2
SWA_Q_ROWS = 2048
MLA_Q_COLS = 512
MLA_K_ROWS = 256
POST_ROWS = 1024
POST_SUB_ROWS = 256
POST_INTERLEAVE = 2
MLA_LOOKAHEAD = 2
SWA_LOOKAHEAD = 2

MLA_ROPE_HALF = MLA_ROPE_DIM // 2
MLA_V_ROWS = MLA_V_DIM + 16
SWA_V_ROWS = HEAD_DIM + 16
LOG2_E = math.log2(math.e)
MLA_Q_SCALE = (MLA_QK_DIM ** -0.5) * LOG2_E
SWA_Q_SCALE = (HEAD_DIM ** -0.5) * LOG2_E

F32 = jnp.float32
BF16 = jnp.bfloat16
NT_DIMS = (((1,), (1,)), ((), ()))


def _rmsnorm_f32(x, g):
    return x * lax.rsqrt(jnp.mean(x * x, axis=-1, keepdims=True) + EPS) * g


def _software_pipeline(units, lookahead, first, middle, last):
    ahead = {i: first(units[i]) for i in range(min(lookahead, len(units)))}
    pending = None
    for i, unit in enumerate(units):
        if i + lookahead < len(units):
            ahead[i + lookahead] = first(units[i + lookahead])
        mid = middle(unit, ahead.pop(i))
        if pending is not None:
            last(pending[0], *pending[1])
        pending = (unit, mid)
    last(pending[0], *pending[1])


def _rope_lane_group(x, cos, sin_signed, half):
    up = pltpu.roll(x, LANES - half, axis=1)
    down = pltpu.roll(x, half, axis=1)
    first = (lax.broadcasted_iota(jnp.int32, x.shape, 1) & (2 * half - 1)) < half
    return x * cos + jnp.where(first, up, down) * sin_signed


def _in_proj_kernel(x_ref, g_ref, win_ref, wqt_ref, qg_ref, wuqt_ref, kvg_ref, wuk_ref, wuvt_ref,
                    swa_tab, swat_tab, mk_tab, mqt_tab,
                    qat_ref, ka_ref, vat_ref, qt_ref, km_ref, vt_ref):
    n_lat = Q_LORA_RANK + KV_LORA_RANK
    n_first = n_lat + LANES
    o_v, o_kr = 0, LANES
    sub = MLA_K_ROWS
    ones = jnp.ones((MLA_V_ROWS - MLA_V_DIM, sub), F32)
    q_pad = jnp.zeros((LANES - MLA_QK_DIM, sub), F32)
    r0, r1, r2 = MLA_NOPE_DIM, MLA_NOPE_DIM + MLA_ROPE_HALF, MLA_QK_DIM
    half = HEAD_DIM // 2

    def project(t):
        rs = slice(t * sub, (t + 1) * sub)
        h = _rmsnorm_f32(x_ref[rs, :], g_ref[...]).astype(BF16)
        lat = jnp.dot(h, win_ref[:, :n_first], preferred_element_type=F32)
        qat = lax.dot_general(wqt_ref[...], h, NT_DIMS, preferred_element_type=F32)
        p = jnp.dot(h, win_ref[:, n_first:], preferred_element_type=F32)
        return lat, qat, p

    def up_project(lat):
        cq = _rmsnorm_f32(lat[:, :Q_LORA_RANK], qg_ref[...]).astype(BF16)
        ckv = _rmsnorm_f32(lat[:, Q_LORA_RANK:n_lat], kvg_ref[...]).astype(BF16)
        qt = lax.dot_general(wuqt_ref[...], cq, NT_DIMS, preferred_element_type=F32)
        kb = jnp.dot(ckv, wuk_ref[...], preferred_element_type=F32)
        vt = lax.dot_general(wuvt_ref[...], ckv, NT_DIMS, preferred_element_type=F32)
        return qt, kb, vt

    def finish(t, lat, qat, p, qt, kb, vt):
        rs = slice(t * sub, (t + 1) * sub)
        cos_a, sin_a = swat_tab[0, :, rs], swat_tab[1, :, rs]
        for hd in range(SWA_HEADS):
            x1, x2 = qat[hd * HEAD_DIM:hd * HEAD_DIM + half], qat[hd * HEAD_DIM + half:(hd + 1) * HEAD_DIM]
            qat_ref[hd, :, rs] = jnp.concatenate(
                [x1 * cos_a - x2 * sin_a, x2 * cos_a + x1 * sin_a], axis=0).astype(BF16)
        ka_ref[rs, :] = _rope_lane_group(lat[:, n_lat:], swa_tab[0, rs, :], swa_tab[1, rs, :], half).astype(BF16)
        vat = p[:, o_v:o_v + LANES].T
        for g in range(SWA_KV_HEADS):
            for i in range(sub // BLOCK):
                vat_ref[g, t * (sub // BLOCK) + i] = jnp.concatenate(
                    [vat[g * HEAD_DIM:(g + 1) * HEAD_DIM, i * BLOCK:(i + 1) * BLOCK], ones[:, :BLOCK]],
                    axis=0).astype(BF16)

        cos_q, sin_q = mqt_tab[0, :, rs], mqt_tab[1, :, rs]
        for hd in range(MLA_HEADS):
            qh = qt[hd * MLA_QK_DIM:(hd + 1) * MLA_QK_DIM]
            x1, x2 = qh[r0:r1], qh[r1:r2]
            qt_ref[hd, :, rs] = jnp.concatenate(
                [qh[:r0] * MLA_Q_SCALE, x1 * cos_q - x2 * sin_q, x2 * cos_q + x1 * sin_q, q_pad],
                axis=0).astype(BF16)

        kr = _rope_lane_group(p[:, o_kr:o_kr + LANES], mk_tab[0, rs, :], mk_tab[1, rs, :], MLA_ROPE_HALF)
        nope_lanes = lax.broadcasted_iota(jnp.int32, kr.shape, 1) < MLA_NOPE_DIM
        for hd in range(MLA_HEADS):
            pair = kb[:, (hd // 2) * LANES:(hd // 2 + 1) * LANES]
            if hd % 2:
                pair = pltpu.roll(pair, MLA_NOPE_DIM, axis=1)
            km_ref[hd, rs, :] = jnp.where(nope_lanes, pair, kr).astype(BF16)
            vt_ref[hd, t] = jnp.concatenate(
                [vt[hd * MLA_V_DIM:(hd + 1) * MLA_V_DIM], ones], axis=0).astype(BF16)

    for t0 in range(0, x_ref.shape[0] // sub, PROJ_INTERLEAVE):
        ts = range(t0, t0 + PROJ_INTERLEAVE)
        first = [project(t) for t in ts]
        second = [up_project(lat) for lat, _, _ in first]
        for t, a, b in zip(ts, first, second):
            finish(t, *a, *b)


def _in_proj(x, g, win, wqt, qg, wuqt, kvg, wuk, wuvt, swa_tab, swat_tab, mk_tab, mqt_tab):
    B, S, D = x.shape
    rows = PROJ_ROWS
    ns = S // rows
    kc = rows // MLA_K_ROWS
    H = MLA_HEADS
    G = SWA_KV_HEADS
    tok = lambda w: pl.BlockSpec((None, rows, w), lambda si, b: (b, si, 0))
    const = lambda a: pl.BlockSpec(a.shape, lambda si, b: (0,) * a.ndim)
    tab = pl.BlockSpec((2, rows, LANES), lambda si, b: (0, si, 0))
    tabt = lambda half: pl.BlockSpec((2, half, rows), lambda si, b: (0, 0, si))
    return pl.pallas_call(
        _in_proj_kernel,
        grid=(ns, B),
        in_specs=[tok(D), const(g), const(win), const(wqt), const(qg), const(wuqt), const(kvg), const(wuk),
                  const(wuvt), tab, tabt(HEAD_DIM // 2), tab, tabt(MLA_ROPE_HALF)],
        out_specs=[pl.BlockSpec((None, SWA_HEADS, HEAD_DIM, rows), lambda si, b: (b, 0, 0, si)),
                   tok(LANES),
                   pl.BlockSpec((None, G, rows // BLOCK, SWA_V_ROWS, BLOCK), lambda si, b: (b, 0, si, 0, 0)),
                   pl.BlockSpec((None, H, LANES, rows), lambda si, b: (b, 0, 0, si)),
                   pl.BlockSpec((None, H, rows, LANES), lambda si, b: (b, 0, si, 0)),
                   pl.BlockSpec((None, H, kc, MLA_V_ROWS, MLA_K_ROWS), lambda si, b: (b, 0, si, 0, 0))],
        out_shape=[jax.ShapeDtypeStruct((B, SWA_HEADS, HEAD_DIM, S), BF16),
                   jax.ShapeDtypeStruct((B, S, LANES), BF16),
                   jax.ShapeDtypeStruct((B, G, S // BLOCK, SWA_V_ROWS, BLOCK), BF16),
                   jax.ShapeDtypeStruct((B, H, LANES, S), BF16),
                   jax.ShapeDtypeStruct((B, H, S, LANES), BF16),
                   jax.ShapeDtypeStruct((B, H, S // MLA_K_ROWS, MLA_V_ROWS, MLA_K_ROWS), BF16)],
        compiler_params=pltpu.CompilerParams(
            dimension_semantics=("arbitrary", "arbitrary"), vmem_limit_bytes=VMEM_LIMIT_BYTES),
        name="in_proj",
    )(x, g, win, wqt, qg, wuqt, kvg, wuk, wuvt, swa_tab, swat_tab, mk_tab, mqt_tab)


def _swa_kernel(sink_ref, qt_ref, k_ref, vt_ref, *refs, n_cast):
    o_ref = refs[n_cast]
    for src, dst in zip(refs[:n_cast], refs[n_cast + 1:]):
        dst[...] = src[:, src.shape[1] - dst.shape[1]:].astype(BF16)
    n = pl.program_id(1)
    n_blk = SWA_Q_ROWS // BLOCK
    width = SWA_GROUP * BLOCK
    key = lax.broadcasted_iota(jnp.int32, (2 * BLOCK, width), 0)
    qry = lax.broadcasted_iota(jnp.int32, (2 * BLOCK, width), 1) & (BLOCK - 1)
    band = (key > qry) & (key <= qry + BLOCK)
    bias = jnp.where(band, 0.0, -jnp.inf).astype(F32)
    zeros = jnp.zeros((HEAD_DIM, width), BF16)

    def scores(unit):
        j, g = unit
        blk = n * n_blk + j
        prev0 = pl.multiple_of(jnp.maximum(blk - 1, 0) * BLOCK, BLOCK)
        cur0 = pl.multiple_of(blk * BLOCK, BLOCK)
        kw = jnp.concatenate([k_ref[pl.ds(prev0, BLOCK), :], k_ref[pl.ds(cur0, BLOCK), :]], axis=0)
        q = jnp.concatenate([qt_ref[SWA_GROUP * g + i, :, j * BLOCK:(j + 1) * BLOCK]
                             for i in range(SWA_GROUP)], axis=1)
        q = jnp.concatenate([q, zeros] if g == 0 else [zeros, q], axis=0)
        s = jnp.dot(kw, q, preferred_element_type=F32) + bias
        if j == 0:
            s = jnp.where((key >= BLOCK) | (blk > 0), s, -jnp.inf)
        return s

    def weighted_values(unit, p, m):
        j, g = unit
        blk = n * n_blk + j
        vw = jnp.concatenate([vt_ref[g, jnp.maximum(blk - 1, 0)], vt_ref[g, blk]], axis=1)
        o = jnp.dot(vw, p, preferred_element_type=F32)
        denom = o[HEAD_DIM:HEAD_DIM + 1] + jnp.exp2(sink_ref[g] - m)
        return o[:HEAD_DIM] / denom

    def store(j, outs):
        ot = jnp.concatenate([outs[g][:, i * BLOCK:(i + 1) * BLOCK]
                              for g in range(SWA_KV_HEADS) for i in range(SWA_GROUP)], axis=0)
        o_ref[j * BLOCK:(j + 1) * BLOCK, :] = ot.T.astype(BF16)

    def softmax(unit, s):
        m = jnp.maximum(jnp.max(s, axis=0, keepdims=True), sink_ref[unit[1]])
        return jnp.exp2(s - m).astype(BF16), m

    outs = {}

    def finish(unit, p, m):
        j, g = unit
        outs[g] = weighted_values(unit, p, m)
        if g == SWA_KV_HEADS - 1:
            store(j, outs)

    _software_pipeline([(j, g) for j in range(n_blk) for g in range(SWA_KV_HEADS)],
                       SWA_LOOKAHEAD, scores, softmax, finish)


def _swa_attention(sink_rows, qat, ka, vat, riders):
    B, H, _, S = qat.shape
    G = SWA_KV_HEADS
    rows = SWA_Q_ROWS
    per_batch = S // rows
    steps = B * per_batch

    def slab(n_rows, n_cols):
        assert n_rows % (steps * BF16_SUBLANES) == 0, n_rows
        return pl.BlockSpec((n_rows // steps, n_cols), lambda b, n: (b * per_batch + n, 0))

    out = pl.pallas_call(
        functools.partial(_swa_kernel, n_cast=len(riders)),
        grid=(B, per_batch),
        in_specs=[pl.BlockSpec(sink_rows.shape, lambda b, n: (0, 0, 0)),
                  pl.BlockSpec((None, H, HEAD_DIM, rows), lambda b, n: (b, 0, 0, n)),
                  pl.BlockSpec((None, S, LANES), lambda b, n: (b, 0, 0)),
                  pl.BlockSpec((None, G, S // BLOCK, SWA_V_ROWS, BLOCK), lambda b, n: (b, 0, 0, 0, 0))]
                 + [slab(*w.shape) for w, _ in riders],
        out_specs=[pl.BlockSpec((None, rows, H * HEAD_DIM), lambda b, n: (b, n, 0))]
                  + [slab(w.shape[0], keep) for w, keep in riders],
        out_shape=[jax.ShapeDtypeStruct((B, S, H * HEAD_DIM), BF16)]
                  + [jax.ShapeDtypeStruct((w.shape[0], keep), BF16) for w, keep in riders],
        compiler_params=pltpu.CompilerParams(
            dimension_semantics=("arbitrary", "arbitrary"), vmem_limit_bytes=VMEM_LIMIT_BYTES),
        name="swa_attention",
    )(sink_rows, qat, ka, vat, *[w for w, _ in riders])
    return out[0], out[1:]


def _mla_kernel(qt_ref, k_ref, vt_ref, o_ref, m_ref, acc_ref):
    n = pl.program_id(1)
    tq, tk = MLA_Q_COLS, MLA_K_ROWS
    per_tile = tq // tk

    m_ref[...] = jnp.full(m_ref.shape, -jnp.inf, F32)
    acc_ref[...] = jnp.zeros(acc_ref.shape, F32)
    causal_bias = jnp.where(lax.broadcasted_iota(jnp.int32, (tk, tq), 0)
                            <= lax.broadcasted_iota(jnp.int32, (tk, tq), 1), 0.0, -jnp.inf).astype(F32)

    def scores(unit):
        j, h, c0, masked = unit
        k0 = pl.multiple_of(j * tk, tk)
        s = jnp.dot(k_ref[h, pl.ds(k0, tk), :], qt_ref[h, :, c0:], preferred_element_type=F32)
        if masked:
            s = s + causal_bias[:, :tq - c0]
        return s

    def weighted_values(unit, p, alpha):
        j, h, c0, masked = unit
        acc_ref[h, :, c0:] = alpha * acc_ref[h, :, c0:] + jnp.dot(vt_ref[h, j], p, preferred_element_type=F32)
        if masked and c0 == (per_tile - 1) * tk and h % 2 == 1:
            pair = jnp.concatenate([acc_ref[i, :MLA_V_DIM] / acc_ref[i, MLA_V_DIM:MLA_V_DIM + 1]
                                    for i in (h - 1, h)], axis=0)
            o_ref[:, (h - 1) * MLA_V_DIM:(h + 1) * MLA_V_DIM] = pair.astype(BF16).T

    def softmax(unit, s):
        _, h, c0, _ = unit
        m_prev = m_ref[h, :, c0:]
        m_new = jnp.maximum(m_prev, jnp.max(s, axis=0, keepdims=True))
        m_ref[h, :, c0:] = m_new
        return jnp.exp2(s - m_new).astype(BF16), jnp.exp2(m_prev - m_new)

    def run(units):
        pairs = [units[i:i + 2] for i in range(0, len(units), 2)]
        _software_pipeline(
            pairs, 1,
            lambda us: [scores(u) for u in us],
            lambda us, ss: ([softmax(u, s) for u, s in zip(us, ss)],),
            lambda us, mids: [weighted_values(u, *m) for u, m in zip(us, mids)])

    def below(i):
        return [(i * per_tile + d, h, 0, False) for d in range(per_tile) for h in range(MLA_HEADS)]

    diagonal = [(n * per_tile + d, h, d * tk, True) for d in range(per_tile) for h in range(MLA_HEADS)]

    trips = jnp.maximum(n - 1, 0) // 2

    def body(i, carry):
        run(below(2 * i) + below(2 * i + 1))
        return carry

    lax.fori_loop(0, trips, body, 0)
    for left in range(3):
        @pl.when(n - 2 * trips == left)
        def _():
            run([u for r in range(left) for u in below(2 * trips + r)] + diagonal)


def _mla_attention(qt, km, vt):
    B, H, _, S = qt.shape
    tq, tk = MLA_Q_COLS, MLA_K_ROWS
    assert tq % tk == 0
    return pl.pallas_call(
        _mla_kernel,
        grid=(B, S // tq),
        in_specs=[pl.BlockSpec((None, H, LANES, tq), lambda b, n: (b, 0, 0, n)),
                  pl.BlockSpec((None, H, S, LANES), lambda b, n: (b, 0, 0, 0)),
                  pl.BlockSpec((None, H, S // tk, MLA_V_ROWS, tk), lambda b, n: (b, 0, 0, 0, 0))],
        out_specs=pl.BlockSpec((None, tq, H * MLA_V_DIM), lambda b, n: (b, n, 0)),
        out_shape=jax.ShapeDtypeStruct((B, S, H * MLA_V_DIM), BF16),
        scratch_shapes=[pltpu.VMEM((H, 1, tq), F32),
                        pltpu.VMEM((H, MLA_V_ROWS, tq), F32)],
        compiler_params=pltpu.CompilerParams(
            dimension_semantics=("arbitrary", "arbitrary"), vmem_limit_bytes=VMEM_LIMIT_BYTES),
        name="mla_attention",
    )(qt, km, vt)


def _post_kernel(x_ref, oa_ref, ob_ref, g1_ref, wg_ref, woa_ref, wob_ref, wout_ref,
                 g2_ref, wgate_ref, wup_ref, wdown_ref, g3_ref, out_ref, *, final_norm):
    D = x_ref.shape[-1]
    dot = functools.partial(jnp.dot, preferred_element_type=F32)
    n_sub = x_ref.shape[0] // POST_SUB_ROWS
    for t0 in range(0, n_sub, POST_INTERLEAVE):
        tiles = [slice((t0 + i) * POST_SUB_ROWS, (t0 + i + 1) * POST_SUB_ROWS) for i in range(POST_INTERLEAVE)]
        xs = [x_ref[rs, :] for rs in tiles]
        hs = [_rmsnorm_f32(x, g1_ref[...]).astype(BF16) for x in xs]
        gates = [dot(h, wg_ref[...]) for h in hs]
        yas = [dot(oa_ref[rs, :], woa_ref[...]) for rs in tiles]
        ybs = [dot(ob_ref[rs, :], wob_ref[...]) for rs in tiles]
        ys = [(jax.nn.sigmoid(g[:, :D]) * ya + jax.nn.sigmoid(g[:, D:]) * yb).astype(BF16)
              for g, ya, yb in zip(gates, yas, ybs)]
        x1s = [x + dot(y, wout_ref[...]) for x, y in zip(xs, ys)]

        h2s = [_rmsnorm_f32(x1, g2_ref[...]).astype(BF16) for x1 in x1s]
        hgs = [dot(h2, wgate_ref[...]) for h2 in h2s]
        hus = [dot(h2, wup_ref[...]) for h2 in h2s]
        acts = [(hg * jax.nn.sigmoid(hg) * hu).astype(BF16) for hg, hu in zip(hgs, hus)]
        x2s = [x1 + dot(a, wdown_ref[...]) for x1, a in zip(x1s, acts)]
        for rs, x2 in zip(tiles, x2s):
            out_ref[rs, :] = _rmsnorm_f32(x2, g3_ref[...]) if final_norm else x2


def _post(x, oa, ob, g1, wg, woa, wob, wout, g2, wgate, wup, wdown, g3, final_norm):
    B, S, D = x.shape
    rows = POST_ROWS
    tok = lambda w: pl.BlockSpec((None, rows, w), lambda b, i: (b, i, 0))
    const = lambda a: pl.BlockSpec(a.shape, lambda b, i: (0,) * a.ndim, pipeline_mode=pl.Buffered(1))
    return pl.pallas_call(
        functools.partial(_post_kernel, final_norm=final_norm),
        grid=(B, S // rows),
        in_specs=[tok(D), tok(oa.shape[-1]), tok(ob.shape[-1])]
                 + [const(a) for a in (g1, wg, woa, wob, wout, g2, wgate, wup, wdown, g3)],
        out_specs=tok(D),
        out_shape=jax.ShapeDtypeStruct((B, S, D), F32),
        compiler_params=pltpu.CompilerParams(
            dimension_semantics=("arbitrary", "arbitrary"), vmem_limit_bytes=VMEM_LIMIT_BYTES),
        name="post_attention",
    )(x, oa, ob, g1, wg, woa, wob, wout, g2, wgate, wup, wdown, g3)


def _rope_angles_t(seq, dim):
    inv = ROPE_THETA ** (-jnp.arange(0, dim, 2, dtype=F32) / dim)
    ang = inv[:, None] * jnp.arange(seq, dtype=F32)[None, :]
    return jnp.cos(ang), jnp.sin(ang)


def _rope_table(cos_t, sin_t, lead, trail):
    cos, sin = cos_t.T, sin_t.T
    seq, half = cos.shape
    reps = (LANES - lead - trail) // (2 * half)
    c = [jnp.ones((seq, lead), F32)] + [cos] * (2 * reps) + [jnp.ones((seq, trail), F32)]
    s = [jnp.zeros((seq, lead), F32)] + [-sin, sin] * reps + [jnp.zeros((seq, trail), F32)]
    return jnp.stack([jnp.concatenate(c, axis=1), jnp.concatenate(s, axis=1)])


def kernel(x, mix_norm_g, w_in, swa_sinks, q_norm_g, w_uq, kv_norm_g, w_ukv, w_o_swa, w_o_mla,
           w_out, ffn_norm_g, w_gate, w_up, w_down, final_norm_g):
    B, S, D = x.shape
    depth = w_in.shape[0]
    assert S % SWA_Q_ROWS == 0 and S % PROJ_ROWS == 0 and S % POST_ROWS == 0 and S % MLA_Q_COLS == 0
    assert PROJ_ROWS % MLA_K_ROWS == 0

    cos_a, sin_a, cos_m, sin_m = lax.optimization_barrier(
        _rope_angles_t(S, HEAD_DIM) + _rope_angles_t(S, MLA_ROPE_DIM))
    swa_tab = _rope_table(cos_a, sin_a, 0, 0)
    mk_tab = _rope_table(cos_m, sin_m, MLA_NOPE_DIM, LANES - MLA_QK_DIM)
    swat_tab = jnp.stack([cos_a, sin_a]) * SWA_Q_SCALE
    mqt_tab = jnp.stack([cos_m, sin_m]) * MLA_Q_SCALE
    row = lambda v: v.reshape(1, -1)

    for l in range(depth):
        wi = w_in[l]
        wkr = jnp.pad(wi[:, 1408:1440], ((0, 0), (MLA_NOPE_DIM, LANES - MLA_QK_DIM)))
        win = jnp.concatenate([wi[:, 768:1408], wi[:, 512:768], wkr], axis=1).astype(BF16)
        wqt = wi[:, :512].T.astype(BF16)
        wg = wi[:, 1440:].astype(BF16)
        sink_rows = jnp.repeat(swa_sinks[l].reshape(SWA_KV_HEADS, 1, SWA_GROUP) * LOG2_E, BLOCK, axis=2)
        wuqt = w_uq[l].T.astype(BF16)
        wkv = w_ukv[l].reshape(KV_LORA_RANK, MLA_HEADS, MLA_NOPE_DIM + MLA_V_DIM)
        wuk = wkv[:, :, :MLA_NOPE_DIM].reshape(KV_LORA_RANK, -1).astype(BF16)
        wuvt = wkv[:, :, MLA_NOPE_DIM:].reshape(KV_LORA_RANK, -1).T.astype(BF16)

        qat, ka, vat, qt, km, vt = _in_proj(x, row(mix_norm_g[l]), win, wqt, row(q_norm_g[l]), wuqt,
                                            row(kv_norm_g[l]), wuk, wuvt, swa_tab, swat_tab, mk_tab, mqt_tab)
        whole = lambda w: (w, w.shape[1])
        oa, (woa, wob, wout, wgate, wup, wdown) = _swa_attention(
            sink_rows, qat, ka, vat, [whole(w[l]) for w in (w_o_swa, w_o_mla, w_out, w_gate, w_up, w_down)])
        ob = _mla_attention(qt, km, vt)
        x = _post(x, oa, ob, row(mix_norm_g[l]), wg, woa, wob, wout, row(ffn_norm_g[l]), wgate, wup, wdown,
                  row(final_norm_g), final_norm=(l == depth - 1))
    return x
```
